```python
import jax, jax.numpy as jnp
from jax import lax
import numpy as np

D_MODEL = 1024
BATCH = 8
SEQ = 2048
DEPTH = 2
DEC_BATCH = 128
DEC_SEQ = 4
PAST_LEN = 16384
PAGE_SIZE = 128

EXPAND = 2
EXPAND_WIDTH = EXPAND * D_MODEL
CONV_WIDTH = 3
POOL_WINDOWS = (2, 4, 8, 16)
N_POOL_GROUPS = len(POOL_WINDOWS)
POOL_GROUP_WIDTH = EXPAND_WIDTH // N_POOL_GROUPS
POOL_HIST = max(POOL_WINDOWS) - 1
N_MIXERS = 2
N_CONV_LAYERS = (DEPTH + 1) // 2
N_POOL_LAYERS = DEPTH // 2
RMS_EPS = 1e-6

kernel_name = "hybrid_shortconv_pool_decoder_step"


def rmsnorm(x, g):
    xf = x.astype(jnp.float32)
    r = lax.rsqrt(jnp.mean(xf * xf, axis=-1, keepdims=True) + RMS_EPS)
    return (xf * r * g.astype(jnp.float32)).astype(x.dtype)


def conv_mixer(h, hist, w_in, conv_w, conv_b, w_out):
    T = h.shape[1]
    proj = h @ w_in
    gb, gc, v, z = jnp.split(proj, 4, axis=-1)
    cv = gc * v
    full = jnp.concatenate([hist.astype(cv.dtype), cv], axis=1)
    conv = conv_b
    for k in range(CONV_WIDTH):
        conv = conv + full[:, k:k + T] * conv_w[k]
    y = gb * conv * jax.nn.silu(z)
    out = y @ w_out
    new_hist = full[:, -(CONV_WIDTH - 1):]
    return out, new_hist


def pool_mixer(h, hist, start_pos, w_in, w_grp, scale, w_out):
    b, T, _ = h.shape
    proj = h @ w_in
    u, z = jnp.split(proj, 2, axis=-1)
    full = jnp.concatenate([hist.astype(u.dtype), u], axis=1)
    fullf = full.astype(jnp.float32)
    cs = jnp.concatenate([jnp.zeros((b, 1, EXPAND_WIDTH), jnp.float32),
                          jnp.cumsum(fullf, axis=1)], axis=1)
    P = POOL_HIST
    pos = (start_pos + jnp.arange(T)).astype(jnp.float32)
    diffs = []
    for g, w in enumerate(POOL_WINDOWS):
        sl = slice(g * POOL_GROUP_WIDTH, (g + 1) * POOL_GROUP_WIDTH)
        win = cs[:, P + 1:P + 1 + T, sl] - cs[:, P + 1 - w:P + 1 - w + T, sl]
        cnt = jnp.minimum(jnp.float32(w), pos + 1.0)
        diffs.append(win / cnt[None, :, None] - fullf[:, P:, sl])
    pooled = jnp.stack(diffs, axis=2).astype(u.dtype)
    mixed = jnp.einsum('btgc,gcd->btgd', pooled, w_grp).reshape(b, T, EXPAND_WIDTH)
    y = mixed * scale * jax.nn.silu(z)
    out = y @ w_out
    new_hist = full[:, -P:]
    return out, new_hist


def _stack(lst, b, rows):
    if lst:
        return jnp.stack(lst)
    return jnp.zeros((0, b, rows, EXPAND_WIDTH), jnp.float32)


def trunk(x, conv_hist, pool_hist, start_pos, norm_g, final_norm_g,
          conv_w_in, conv_w, conv_b, conv_w_out,
          pool_w_in, pool_w_grp, pool_scale, pool_w_out):
    b = x.shape[0]
    new_conv, new_pool = [], []
    ia, ib = 0, 0
    for i in range(DEPTH):
        hn = rmsnorm(x, norm_g[i])
        if i % N_MIXERS == 0:
            out, nh = conv_mixer(hn, conv_hist[ia], conv_w_in[ia], conv_w[ia],
                                 conv_b[ia], conv_w_out[ia])
            new_conv.append(nh)
            ia += 1
        else:
            out, nh = pool_mixer(hn, pool_hist[ib], start_pos, pool_w_in[ib],
                                 pool_w_grp[ib], pool_scale[ib], pool_w_out[ib])
            new_pool.append(nh)
            ib += 1
        x = x + out
    return (rmsnorm(x, final_norm_g), _stack(new_conv, b, CONV_WIDTH - 1),
            _stack(new_pool, b, POOL_HIST))


def setup_inputs(seed: int = 0) -> dict:
    key = jax.random.key(seed)
    ks = jax.random.split(key, 14)
    D, E, Gc = D_MODEL, EXPAND_WIDTH, POOL_GROUP_WIDTH
    nrm = jax.random.normal
    return {
        "x_prompt": nrm(ks[0], (BATCH, SEQ, D), jnp.float32),
        "x_sample": nrm(ks[1], (DEC_BATCH, DEC_SEQ, D), jnp.float32),
        "state_conv": nrm(ks[2], (N_CONV_LAYERS, DEC_BATCH, CONV_WIDTH - 1, E), jnp.float32),
        "state_pool": nrm(ks[3], (N_POOL_LAYERS, DEC_BATCH, POOL_HIST, E), jnp.float32),
        "norm_g": 1.0 + 0.02 * nrm(ks[4], (DEPTH, D), jnp.float32),
        "final_norm_g": 1.0 + 0.02 * nrm(ks[5], (D,), jnp.float32),
        "conv_w_in": nrm(ks[6], (N_CONV_LAYERS, D, 4 * E), jnp.float32) * D ** -0.5,
        "conv_w": nrm(ks[7], (N_CONV_LAYERS, CONV_WIDTH, E), jnp.float32) * CONV_WIDTH ** -0.5,
        "conv_b": 0.02 * nrm(ks[8], (N_CONV_LAYERS, E), jnp.float32),
        "conv_w_out": nrm(ks[9], (N_CONV_LAYERS, E, D), jnp.float32) * E ** -0.5,
        "pool_w_in": nrm(ks[10], (N_POOL_LAYERS, D, 2 * E), jnp.float32) * D ** -0.5,
        "pool_w_grp": nrm(ks[11], (N_POOL_LAYERS, N_POOL_GROUPS, Gc, Gc), jnp.float32) * Gc ** -0.5,
        "pool_scale": 1.0 + 0.02 * nrm(ks[12], (N_POOL_LAYERS, E), jnp.float32),
        "pool_w_out": nrm(ks[13], (N_POOL_LAYERS, E, D), jnp.float32) * E ** -0.5,
    }


def reference(x_prompt, x_sample, state_conv, state_pool, norm_g, final_norm_g,
              conv_w_in, conv_w, conv_b, conv_w_out,
              pool_w_in, pool_w_grp, pool_scale, pool_w_out):
    b = x_prompt.shape[0]
    conv_hist0 = jnp.zeros((N_CONV_LAYERS, b, CONV_WIDTH - 1, EXPAND_WIDTH), x_prompt.dtype)
    pool_hist0 = jnp.zeros((N_POOL_LAYERS, b, POOL_HIST, EXPAND_WIDTH), x_prompt.dtype)
    y_prompt, new_conv_prompt, new_pool_prompt = trunk(
        x_prompt, conv_hist0, pool_hist0, 0, norm_g, final_norm_g,
        conv_w_in, conv_w, conv_b, conv_w_out,
        pool_w_in, pool_w_grp, pool_scale, pool_w_out)
    y_sample, new_conv_sample, new_pool_sample = trunk(
        x_sample, state_conv, state_pool, PAST_LEN, norm_g, final_norm_g,
        conv_w_in, conv_w, conv_b, conv_w_out,
        pool_w_in, pool_w_grp, pool_scale, pool_w_out)
    return (y_prompt, y_sample, new_conv_prompt, new_conv_sample,
            new_pool_prompt, new_pool_sample)
```

```python
import functools

import jax
import jax.numpy as jnp
from jax import lax
from jax.experimental import pallas as pl
from jax.experimental.pallas import tpu as pltpu

D_MODEL = 1024
EXPAND_WIDTH = 2048
CONV_WIDTH = 3
POOL_WINDOWS = (2, 4, 8, 16)
N_POOL_GROUPS = len(POOL_WINDOWS)
POOL_GROUP_WIDTH = EXPAND_WIDTH // N_POOL_GROUPS
POOL_HIST = max(POOL_WINDOWS) - 1
PAST_LEN = 16384
RMS_EPS = 1e-6

LANES = 128
SUBLANES = 8
CONV_CARRY_ROWS = SUBLANES
POOL_CARRY_ROWS = 2 * SUBLANES
CHUNK = POOL_GROUP_WIDTH
N_CHUNKS = EXPAND_WIDTH // CHUNK
PROMPT_TILE = 256
VMEM_PHYSICAL_BYTES = 64 * 1024 * 1024

BF16 = jnp.bfloat16
F32 = jnp.float32


def _dot(a, b):
    return jnp.dot(a, b, preferred_element_type=F32)


def _rmsnorm(x, g):
    r = lax.rsqrt(jnp.mean(x * x, axis=-1, keepdims=True) + RMS_EPS)
    return x * r * g


def _silu(z):
    return z / (1.0 + jnp.exp(-z))


def _resident(shape):
    zeros = (0,) * len(shape)
    return pl.BlockSpec(shape, lambda *_: zeros, pipeline_mode=pl.Buffered(1))


def _prompt_kernel(x_ref, g_ref, gf_ref, win0_ref, cw_ref, cb_ref, wout0_ref,
                   win1_ref, wgrp_ref, ps_ref, wout1_ref,
                   y_ref, nconv_ref, npool_ref,
                   cvh_ref, uh_ref, ybuf_ref, *, tm):
    i = pl.program_id(1)
    E = EXPAND_WIDTH

    @pl.when(i == 0)
    def _():
        cvh_ref[...] = jnp.zeros_like(cvh_ref)
        uh_ref[...] = jnp.zeros_like(uh_ref)

    x = x_ref[...]
    hn = _rmsnorm(x, g_ref[0:1, :]).astype(BF16)
    for j in range(N_CHUNKS):
        c = slice(j * CHUNK, (j + 1) * CHUNK)
        gb = _dot(hn, win0_ref[:, j * CHUNK:(j + 1) * CHUNK])
        gc = _dot(hn, win0_ref[:, E + j * CHUNK:E + (j + 1) * CHUNK])
        v = _dot(hn, win0_ref[:, 2 * E + j * CHUNK:2 * E + (j + 1) * CHUNK])
        z = _dot(hn, win0_ref[:, 3 * E + j * CHUNK:3 * E + (j + 1) * CHUNK])
        cv = gc * v
        ext = jnp.concatenate([cvh_ref[:, c], cv], axis=0)
        cm1 = pltpu.roll(ext, 1, 0)[CONV_CARRY_ROWS:]
        cm2 = pltpu.roll(ext, 2, 0)[CONV_CARRY_ROWS:]
        conv = cb_ref[:, c] + cm2 * cw_ref[0:1, c]
        conv = conv + cm1 * cw_ref[1:2, c]
        conv = conv + cv * cw_ref[2:3, c]
        ybuf_ref[:, c] = (gb * conv * _silu(z)).astype(BF16)
        cvh_ref[:, c] = cv[tm - CONV_CARRY_ROWS:]
    x1 = x + _dot(ybuf_ref[...], wout0_ref[...])

    hn1 = _rmsnorm(x1, g_ref[1:2, :]).astype(BF16)
    seen = (i * tm + 1 + lax.broadcasted_iota(jnp.int32, (tm, LANES), 0)).astype(F32)
    for g, w in enumerate(POOL_WINDOWS):
        c = slice(g * CHUNK, (g + 1) * CHUNK)
        u = _dot(hn1, win1_ref[:, g * CHUNK:(g + 1) * CHUNK])
        z = _dot(hn1, win1_ref[:, E + g * CHUNK:E + (g + 1) * CHUNK])
        s = jnp.concatenate([uh_ref[:, c], u], axis=0)
        shift = 1
        while shift < w:
            s = s + pltpu.roll(s, shift, 0)
            shift *= 2
        inv = 1.0 / jnp.minimum(jnp.float32(w), seen)
        inv = jnp.concatenate([inv] * (CHUNK // LANES), axis=1)
        p = s[POOL_CARRY_ROWS:] * inv - u
        q = _dot(p.astype(BF16), wgrp_ref[g])
        ybuf_ref[:, c] = (q * ps_ref[:, c] * _silu(z)).astype(BF16)
        uh_ref[:, c] = u[tm - POOL_CARRY_ROWS:]
    x2 = x1 + _dot(ybuf_ref[...], wout1_ref[...])
    y_ref[...] = _rmsnorm(x2, gf_ref[...])

    @pl.when(i == pl.num_programs(1) - 1)
    def _():
        nconv_ref[...] = cvh_ref[CONV_CARRY_ROWS - (CONV_WIDTH - 1):, :]
        npool_ref[...] = uh_ref[POOL_CARRY_ROWS - POOL_HIST:, :]


def _prompt_call(x, g, gf, win0, cw, cb, wout0, win1, wgrp, ps, wout1):
    B, S, D = x.shape
    E = EXPAND_WIDTH
    tm = PROMPT_TILE
    assert S % tm == 0 and tm >= POOL_CARRY_ROWS
    weight_bytes = 2 * (win0.size + wout0.size + win1.size + wgrp.size + wout1.size)
    tile_bytes = 4 * tm * D * 4
    temp_bytes = 16 * tm * CHUNK * 4 + tm * E * 2
    vmem_limit = min(weight_bytes + tile_bytes + temp_bytes + (4 << 20), VMEM_PHYSICAL_BYTES - (4 << 20))
    return pl.pallas_call(
        functools.partial(_prompt_kernel, tm=tm),
        grid=(B, S // tm),
        in_specs=[
            pl.BlockSpec((None, tm, D), lambda b, i: (b, i, 0)),
            _resident(g.shape), _resident(gf.shape),
            _resident(win0.shape), _resident(cw.shape), _resident(cb.shape), _resident(wout0.shape),
            _resident(win1.shape), _resident(wgrp.shape), _resident(ps.shape), _resident(wout1.shape),
        ],
        out_specs=[
            pl.BlockSpec((None, tm, D), lambda b, i: (b, i, 0)),
            pl.BlockSpec((None, CONV_WIDTH - 1, E), lambda b, i: (b, 0, 0)),
            pl.BlockSpec((None, POOL_HIST, E), lambda b, i: (b, 0, 0)),
        ],
        out_shape=[
            jax.ShapeDtypeStruct((B, S, D), F32),
            jax.ShapeDtypeStruct((B, CONV_WIDTH - 1, E), F32),
            jax.ShapeDtypeStruct((B, POOL_HIST, E), F32),
        ],
        scratch_shapes=[
            pltpu.VMEM((CONV_CARRY_ROWS, E), F32),
            pltpu.VMEM((POOL_CARRY_ROWS, E), F32),
            pltpu.VMEM((tm, E), BF16),
        ],
        compiler_params=pltpu.CompilerParams(
            dimension_semantics=("arbitrary", "arbitrary"),
            vmem_limit_bytes=vmem_limit),
        name="prompt_fused",
    )(x, g, gf, win0, cw, cb, wout0, win1, wgrp, ps, wout1)


def _sample_conv_kernel(x_ref, g_ref, wgb_ref, wgc_ref, wv_ref, wz_ref, hist_ref, cw_ref, cb_ref,
                        wout_ref, x1_ref, nconv_ref, hn_ref, *, nb, nt):
    j = pl.program_id(0)

    @pl.when(j == 0)
    def _():
        x = x_ref[...]
        hn_ref[...] = _rmsnorm(x, g_ref[0:1, :]).astype(BF16)
        x1_ref[...] = x

    hn = hn_ref[...]
    gb = _dot(hn, wgb_ref[...])
    gc = _dot(hn, wgc_ref[...])
    v = _dot(hn, wv_ref[...])
    z = _dot(hn, wz_ref[...])
    cv = gc * v
    full = jnp.concatenate([hist_ref[k] for k in range(CONV_WIDTH - 1)] + [cv], axis=0)
    conv = cb_ref[...]
    for k in range(CONV_WIDTH):
        conv = conv + full[k * nb:(k + nt) * nb] * cw_ref[k:k + 1, :]
    y = (gb * conv * _silu(z)).astype(BF16)
    x1_ref[...] += _dot(y, wout_ref[...])
    for k in range(CONV_WIDTH - 1):
        nconv_ref[k] = full[(nt + k) * nb:(nt + k + 1) * nb]


def _sample_pool_kernel(x_ref, g_ref, gf_ref, wu_ref, wz_ref, wgrp_ref, ps_ref, wout_ref, hist_ref,
                        y_ref, u_ref, hn_ref, acc_ref, p_ref, *, nb, nt, start_pos):
    g = pl.program_id(0)

    @pl.when(g == 0)
    def _():
        x = x_ref[...]
        hn_ref[...] = _rmsnorm(x, g_ref[1:2, :]).astype(BF16)
        acc_ref[...] = x

    hn = hn_ref[...]
    u = _dot(hn, wu_ref[...])
    z = _dot(hn, wz_ref[...])
    u_ref[...] = u

    def full(r):
        if r < POOL_HIST:
            return hist_ref[r]
        return u[(r - POOL_HIST) * nb:(r - POOL_HIST + 1) * nb]

    for gi, w in enumerate(POOL_WINDOWS):
        @pl.when(g == gi)
        def _(w=w):
            for t in range(nt):
                win = full(POOL_HIST + t)
                for k in range(1, w):
                    win = win + full(POOL_HIST + t - k)
                cnt = float(min(w, start_pos + t + 1))
                p_ref[t * nb:(t + 1) * nb, :] = (win / cnt - full(POOL_HIST + t)).astype(BF16)

    q = _dot(p_ref[...], wgrp_ref[...])
    y = (q * ps_ref[...] * _silu(z)).astype(BF16)
    acc_ref[...] += _dot(y, wout_ref[...])

    @pl.when(g == pl.num_programs(0) - 1)
    def _():
        y_ref[...] = _rmsnorm(acc_ref[...], gf_ref[...])


def _sample_calls(x_tm, conv_hist_tm, pool_hist_tm, g, gf, win0, cw, cb, wout0, win1, wgrp, ps, wout1, nb, nt):
    M, D = x_tm.shape
    E = EXPAND_WIDTH
    n = N_CHUNKS
    col = lambda s: pl.BlockSpec((D, CHUNK), lambda j, s=s: (0, s * n + j))
    full2 = lambda a: pl.BlockSpec(a.shape, lambda j: (0, 0))
    x1, nconv_tm = pl.pallas_call(
        functools.partial(_sample_conv_kernel, nb=nb, nt=nt),
        grid=(n,),
        in_specs=[
            full2(x_tm), full2(g),
            col(0), col(1), col(2), col(3),
            pl.BlockSpec((CONV_WIDTH - 1, nb, CHUNK), lambda j: (0, 0, j)),
            pl.BlockSpec((CONV_WIDTH, CHUNK), lambda j: (0, j)),
            pl.BlockSpec((1, CHUNK), lambda j: (0, j)),
            pl.BlockSpec((CHUNK, D), lambda j: (j, 0)),
        ],
        out_specs=[
            pl.BlockSpec((M, D), lambda j: (0, 0)),
            pl.BlockSpec((CONV_WIDTH - 1, nb, CHUNK), lambda j: (0, 0, j)),
        ],
        out_shape=[
            jax.ShapeDtypeStruct((M, D), F32),
            jax.ShapeDtypeStruct((CONV_WIDTH - 1, nb, E), F32),
        ],
        scratch_shapes=[pltpu.VMEM((M, D), BF16)],
        compiler_params=pltpu.CompilerParams(dimension_semantics=("arbitrary",)),
        name="sample_conv",
    )(x_tm, g, win0, win0, win0, win0, conv_hist_tm, cw, cb, wout0)

    y_tm, u_tm = pl.pallas_call(
        functools.partial(_sample_pool_kernel, nb=nb, nt=nt, start_pos=PAST_LEN),
        grid=(n,),
        in_specs=[
            full2(x1), full2(g), full2(gf),
            col(0), col(1),
            pl.BlockSpec((None, CHUNK, CHUNK), lambda j: (j, 0, 0)),
            pl.BlockSpec((1, CHUNK), lambda j: (0, j)),
            pl.BlockSpec((CHUNK, D), lambda j: (j, 0)),
            pl.BlockSpec((POOL_HIST, nb, CHUNK), lambda j: (0, 0, j)),
        ],
        out_specs=[
            pl.BlockSpec((M, D), lambda j: (0, 0)),
            pl.BlockSpec((M, CHUNK), lambda j: (0, j)),
        ],
        out_shape=[
            jax.ShapeDtypeStruct((M, D), F32),
            jax.ShapeDtypeStruct((M, E), F32),
        ],
        scratch_shapes=[
            pltpu.VMEM((M, D), BF16),
            pltpu.VMEM((M, D), F32),
            pltpu.VMEM((M, CHUNK), BF16),
        ],
        compiler_params=pltpu.CompilerParams(dimension_semantics=("arbitrary",)),
        name="sample_pool",
    )(x1, g, gf, win1, win1, wgrp, ps, wout1, pool_hist_tm)
    return y_tm, nconv_tm, u_tm


def kernel(x_prompt, x_sample, state_conv, state_pool, norm_g, final_norm_g,
           conv_w_in, conv_w, conv_b, conv_w_out,
           pool_w_in, pool_w_grp, pool_scale, pool_w_out):
    assert norm_g.shape[0] == 2 and conv_w_in.shape[0] == 1 and pool_w_in.shape[0] == 1
    nb, nt, D = x_sample.shape
    E = EXPAND_WIDTH
    gf = final_norm_g.reshape(1, D)
    win0 = conv_w_in[0].astype(BF16)
    wout0 = conv_w_out[0].astype(BF16)
    win1 = pool_w_in[0].astype(BF16)
    wgrp = pool_w_grp[0].astype(BF16)
    wout1 = pool_w_out[0].astype(BF16)
    cw, cb, ps = conv_w[0], conv_b, pool_scale

    y_prompt, nconv_p, npool_p = _prompt_call(
        x_prompt, norm_g, gf, win0, cw, cb, wout0, win1, wgrp, ps, wout1)

    x_tm = x_sample.transpose(1, 0, 2).reshape(nt * nb, D)
    conv_hist_tm = state_conv[0].transpose(1, 0, 2)
    pool_hist_tm = state_pool[0].transpose(1, 0, 2)
    y_tm, nconv_tm, u_tm = _sample_calls(
        x_tm, conv_hist_tm, pool_hist_tm, norm_g, gf, win0, cw, cb, wout0, win1, wgrp, ps, wout1, nb, nt)
    y_sample = y_tm.reshape(nt, nb, D).transpose(1, 0, 2)
    nconv_s = nconv_tm.transpose(1, 0, 2)
    u_bm = u_tm.reshape(nt, nb, E).transpose(1, 0, 2)
    npool_s = jnp.concatenate([state_pool[0][:, nt:], u_bm], axis=1)

    return (y_prompt, y_sample, nconv_p[None], nconv_s[None], npool_p[None], npool_s[None])
```

```python
import functools

import jax
import jax.numpy as jnp
from jax import lax
from jax.experimental import pallas as pl
from jax.experimental.pallas import tpu as pltpu

D_MODEL = 1024
EXPAND_WIDTH = 2048
CONV_WIDTH = 3
POOL_WINDOWS = (2, 4, 8, 16)
N_POOL_GROUPS = len(POOL_WINDOWS)
POOL_GROUP_WIDTH = EXPAND_WIDTH // N_POOL_GROUPS
POOL_HIST = max(POOL_WINDOWS) - 1
PAST_LEN = 16384
RMS_EPS = 1e-6

LANES = 128
SUBLANES = 8
CONV_CARRY_ROWS = SUBLANES
POOL_CARRY_ROWS = 2 * SUBLANES
CHUNK = POOL_GROUP_WIDTH
N_CHUNKS = EXPAND_WIDTH // CHUNK
PROMPT_TILE = 512
VMEM_PHYSICAL_BYTES = 64 * 1024 * 1024

BF16 = jnp.bfloat16
F32 = jnp.float32


def _dot(a, b):
    return jnp.dot(a, b, preferred_element_type=F32)


def _rmsnorm(x, g):
    r = lax.rsqrt(jnp.mean(x * x, axis=-1, keepdims=True) + RMS_EPS)
    return x * r * g


def _silu(z):
    return z / (1.0 + jnp.exp(-z))


def _resident(shape):
    zeros = (0,) * len(shape)
    return pl.BlockSpec(shape, lambda *_: zeros, pipeline_mode=pl.Buffered(1))


def _prompt_kernel(x_ref, g_ref, gf_ref, win0_ref, cw_ref, cb_ref, wout0_ref,
                   win1_ref, wgrp_ref, ps_ref, wout1_ref,
                   y_ref, nconv_ref, npool_ref,
                   cvh_ref, uh_ref, ybuf_ref, *, tm):
    i = pl.program_id(1)
    E = EXPAND_WIDTH

    @pl.when(i == 0)
    def _():
        cvh_ref[...] = jnp.zeros_like(cvh_ref)
        uh_ref[...] = jnp.zeros_like(uh_ref)

    x = x_ref[...]
    hn = _rmsnorm(x, g_ref[0:1, :]).astype(BF16)
    for j in range(N_CHUNKS):
        c = slice(j * CHUNK, (j + 1) * CHUNK)
        gb = _dot(hn, win0_ref[:, j * CHUNK:(j + 1) * CHUNK])
        gc = _dot(hn, win0_ref[:, E + j * CHUNK:E + (j + 1) * CHUNK])
        v = _dot(hn, win0_ref[:, 2 * E + j * CHUNK:2 * E + (j + 1) * CHUNK])
        z = _dot(hn, win0_ref[:, 3 * E + j * CHUNK:3 * E + (j + 1) * CHUNK])
        cv = gc * v
        ext = jnp.concatenate([cvh_ref[:, c], cv], axis=0)
        cm1 = pltpu.roll(ext, 1, 0)[CONV_CARRY_ROWS:]
        cm2 = pltpu.roll(ext, 2, 0)[CONV_CARRY_ROWS:]
        conv = cb_ref[:, c] + cm2 * cw_ref[0:1, c]
        conv = conv + cm1 * cw_ref[1:2, c]
        conv = conv + cv * cw_ref[2:3, c]
        ybuf_ref[:, c] = (gb * conv * _silu(z)).astype(BF16)
        cvh_ref[:, c] = cv[tm - CONV_CARRY_ROWS:]
    x1 = x + _dot(ybuf_ref[...], wout0_ref[...])

    hn1 = _rmsnorm(x1, g_ref[1:2, :]).astype(BF16)
    seen = (i * tm + 1 + lax.broadcasted_iota(jnp.int32, (tm, LANES), 0)).astype(F32)
    for g, w in enumerate(POOL_WINDOWS):
        c = slice(g * CHUNK, (g + 1) * CHUNK)
        u = _dot(hn1, win1_ref[:, g * CHUNK:(g + 1) * CHUNK])
        z = _dot(hn1, win1_ref[:, E + g * CHUNK:E + (g + 1) * CHUNK])
        s = jnp.concatenate([uh_ref[:, c], u], axis=0)
        shift = 1
        while shift < w:
            s = s + pltpu.roll(s, shift, 0)
            shift *= 2
        inv = 1.0 / jnp.minimum(jnp.float32(w), seen)
        inv = jnp.concatenate([inv] * (CHUNK // LANES), axis=1)
        p = s[POOL_CARRY_ROWS:] * inv - u
        q = _dot(p.astype(BF16), wgrp_ref[g])
        ybuf_ref[:, c] = (q * ps_ref[:, c] * _silu(z)).astype(BF16)
        uh_ref[:, c] = u[tm - POOL_CARRY_ROWS:]
    x2 = x1 + _dot(ybuf_ref[...], wout1_ref[...])
    y_ref[...] = _rmsnorm(x2, gf_ref[...])

    @pl.when(i == pl.num_programs(1) - 1)
    def _():
        nconv_ref[...] = cvh_ref[CONV_CARRY_ROWS - (CONV_WIDTH - 1):, :]
        npool_ref[...] = uh_ref[POOL_CARRY_ROWS - POOL_HIST:, :]


def _prompt_call(x, g, gf, win0, cw, cb, wout0, win1, wgrp, ps, wout1):
    B, S, D = x.shape
    E = EXPAND_WIDTH
    tm = PROMPT_TILE
    assert S % tm == 0 and tm >= POOL_CARRY_ROWS
    weight_bytes = 2 * (win0.size + wout0.size + win1.size + wgrp.size + wout1.size)
    tile_bytes = 4 * tm * D * 4
    temp_bytes = 16 * tm * CHUNK * 4 + tm * E * 2
    vmem_limit = min(weight_bytes + tile_bytes + temp_bytes + (4 << 20), VMEM_PHYSICAL_BYTES - (4 << 20))
    return pl.pallas_call(
        functools.partial(_prompt_kernel, tm=tm),
        grid=(B, S // tm),
        in_specs=[
            pl.BlockSpec((None, tm, D), lambda b, i: (b, i, 0)),
            _resident(g.shape), _resident(gf.shape),
            _resident(win0.shape), _resident(cw.shape), _resident(cb.shape), _resident(wout0.shape),
            _resident(win1.shape), _resident(wgrp.shape), _resident(ps.shape), _resident(wout1.shape),
        ],
        out_specs=[
            pl.BlockSpec((None, tm, D), lambda b, i: (b, i, 0)),
            pl.BlockSpec((None, CONV_WIDTH - 1, E), lambda b, i: (b, 0, 0)),
            pl.BlockSpec((None, POOL_HIST, E), lambda b, i: (b, 0, 0)),
        ],
        out_shape=[
            jax.ShapeDtypeStruct((B, S, D), F32),
            jax.ShapeDtypeStruct((B, CONV_WIDTH - 1, E), F32),
            jax.ShapeDtypeStruct((B, POOL_HIST, E), F32),
        ],
        scratch_shapes=[
            pltpu.VMEM((CONV_CARRY_ROWS, E), F32),
            pltpu.VMEM((POOL_CARRY_ROWS, E), F32),
            pltpu.VMEM((tm, E), BF16),
        ],
        compiler_params=pltpu.CompilerParams(
            dimension_semantics=("arbitrary", "arbitrary"),
            vmem_limit_bytes=vmem_limit),
        name="prompt_fused",
    )(x, g, gf, win0, cw, cb, wout0, win1, wgrp, ps, wout1)


def _sample_conv_kernel(x_ref, g_ref, wgb_ref, wgc_ref, wv_ref, wz_ref, hist_ref, cw_ref, cb_ref,
                        wout_ref, x1_ref, nconv_ref, hn_ref, *, nb, nt):
    j = pl.program_id(0)

    @pl.when(j == 0)
    def _():
        x = x_ref[...]
        hn_ref[...] = _rmsnorm(x, g_ref[0:1, :]).astype(BF16)
        x1_ref[...] = x

    hn = hn_ref[...]
    gb = _dot(hn, wgb_ref[...])
    gc = _dot(hn, wgc_ref[...])
    v = _dot(hn, wv_ref[...])
    z = _dot(hn, wz_ref[...])
    cv = gc * v
    full = jnp.concatenate([hist_ref[k] for k in range(CONV_WIDTH - 1)] + [cv], axis=0)
    conv = cb_ref[...]
    for k in range(CONV_WIDTH):
        conv = conv + full[k * nb:(k + nt) * nb] * cw_ref[k:k + 1, :]
    y = (gb * conv * _silu(z)).astype(BF16)
    x1_ref[...] += _dot(y, wout_ref[...])
    for k in range(CONV_WIDTH - 1):
        nconv_ref[k] = full[(nt + k) * nb:(nt + k + 1) * nb]


def _sample_pool_kernel(x_ref, g_ref, gf_ref, wu_ref, wz_ref, wgrp_ref, ps_ref, wout_ref, hist_ref,
                        y_ref, u_ref, hn_ref, acc_ref, p_ref, *, nb, nt, start_pos):
    g = pl.program_id(0)

    @pl.when(g == 0)
    def _():
        x = x_ref[...]
        hn_ref[...] = _rmsnorm(x, g_ref[1:2, :]).astype(BF16)
        acc_ref[...] = x

    hn = hn_ref[...]
    u = _dot(hn, wu_ref[...])
    z = _dot(hn, wz_ref[...])
    u_ref[...] = u

    def full(r):
        if r < POOL_HIST:
            return hist_ref[r]
        return u[(r - POOL_HIST) * nb:(r - POOL_HIST + 1) * nb]

    for gi, w in enumerate(POOL_WINDOWS):
        @pl.when(g == gi)
        def _(w=w):
            for t in range(nt):
                win = full(POOL_HIST + t)
                for k in range(1, w):
                    win = win + full(POOL_HIST + t - k)
                cnt = float(min(w, start_pos + t + 1))
                p_ref[t * nb:(t + 1) * nb, :] = (win / cnt - full(POOL_HIST + t)).astype(BF16)

    q = _dot(p_ref[...], wgrp_ref[...])
    y = (q * ps_ref[...] * _silu(z)).astype(BF16)
    acc_ref[...] += _dot(y, wout_ref[...])

    @pl.when(g == pl.num_programs(0) - 1)
    def _():
        y_ref[...] = _rmsnorm(acc_ref[...], gf_ref[...])


def _sample_calls(x_tm, conv_hist_tm, pool_hist_tm, g, gf, win0, cw, cb, wout0, win1, wgrp, ps, wout1, nb, nt):
    M, D = x_tm.shape
    E = EXPAND_WIDTH
    n = N_CHUNKS
    col = lambda s: pl.BlockSpec((D, CHUNK), lambda j, s=s: (0, s * n + j))
    full2 = lambda a: pl.BlockSpec(a.shape, lambda j: (0, 0))
    x1, nconv_tm = pl.pallas_call(
        functools.partial(_sample_conv_kernel, nb=nb, nt=nt),
        grid=(n,),
        in_specs=[
            full2(x_tm), full2(g),
            col(0), col(1), col(2), col(3),
            pl.BlockSpec((CONV_WIDTH - 1, nb, CHUNK), lambda j: (0, 0, j)),
            pl.BlockSpec((CONV_WIDTH, CHUNK), lambda j: (0, j)),
            pl.BlockSpec((1, CHUNK), lambda j: (0, j)),
            pl.BlockSpec((CHUNK, D), lambda j: (j, 0)),
        ],
        out_specs=[
            pl.BlockSpec((M, D), lambda j: (0, 0)),
            pl.BlockSpec((CONV_WIDTH - 1, nb, CHUNK), lambda j: (0, 0, j)),
        ],
        out_shape=[
            jax.ShapeDtypeStruct((M, D), F32),
            jax.ShapeDtypeStruct((CONV_WIDTH - 1, nb, E), F32),
        ],
        scratch_shapes=[pltpu.VMEM((M, D), BF16)],
        compiler_params=pltpu.CompilerParams(dimension_semantics=("arbitrary",)),
        name="sample_conv",
    )(x_tm, g, win0, win0, win0, win0, conv_hist_tm, cw, cb, wout0)

    y_tm, u_tm = pl.pallas_call(
        functools.partial(_sample_pool_kernel, nb=nb, nt=nt, start_pos=PAST_LEN),
        grid=(n,),
        in_specs=[
            full2(x1), full2(g), full2(gf),
            col(0), col(1),
            pl.BlockSpec((None, CHUNK, CHUNK), lambda j: (j, 0, 0)),
            pl.BlockSpec((1, CHUNK), lambda j: (0, j)),
            pl.BlockSpec((CHUNK, D), lambda j: (j, 0)),
            pl.BlockSpec((POOL_HIST, nb, CHUNK), lambda j: (0, 0, j)),
        ],
        out_specs=[
            pl.BlockSpec((M, D), lambda j: (0, 0)),
            pl.BlockSpec((M, CHUNK), lambda j: (0, j)),
        ],
        out_shape=[
            jax.ShapeDtypeStruct((M, D), F32),
            jax.ShapeDtypeStruct((M, E), F32),
        ],
        scratch_shapes=[
            pltpu.VMEM((M, D), BF16),
            pltpu.VMEM((M, D), F32),
            pltpu.VMEM((M, CHUNK), BF16),
        ],
        compiler_params=pltpu.CompilerParams(dimension_semantics=("arbitrary",)),
        name="sample_pool",
    )(x1, g, gf, win1, win1, wgrp, ps, wout1, pool_hist_tm)
    return y_tm, nconv_tm, u_tm


def kernel(x_prompt, x_sample, state_conv, state_pool, norm_g, final_norm_g,
           conv_w_in, conv_w, conv_b, conv_w_out,
           pool_w_in, pool_w_grp, pool_scale, pool_w_out):
    assert norm_g.shape[0] == 2 and conv_w_in.shape[0] == 1 and pool_w_in.shape[0] == 1
    nb, nt, D = x_sample.shape
    E = EXPAND_WIDTH
    gf = final_norm_g.reshape(1, D)
    win0 = conv_w_in[0].astype(BF16)
    wout0 = conv_w_out[0].astype(BF16)
    win1 = pool_w_in[0].astype(BF16)
    wgrp = pool_w_grp[0].astype(BF16)
    wout1 = pool_w_out[0].astype(BF16)
    cw, cb, ps = conv_w[0], conv_b, pool_scale

    y_prompt, nconv_p, npool_p = _prompt_call(
        x_prompt, norm_g, gf, win0, cw, cb, wout0, win1, wgrp, ps, wout1)

    x_tm = x_sample.transpose(1, 0, 2).reshape(nt * nb, D)
    conv_hist_tm = state_conv[0].transpose(1, 0, 2)
    pool_hist_tm = state_pool[0].transpose(1, 0, 2)
    y_tm, nconv_tm, u_tm = _sample_calls(
        x_tm, conv_hist_tm, pool_hist_tm, norm_g, gf, win0, cw, cb, wout0, win1, wgrp, ps, wout1, nb, nt)
    y_sample = y_tm.reshape(nt, nb, D).transpose(1, 0, 2)
    nconv_s = nconv_tm.transpose(1, 0, 2)
    u_bm = u_tm.reshape(nt, nb, E).transpose(1, 0, 2)
    npool_s = jnp.concatenate([state_pool[0][:, nt:], u_bm], axis=1)

    return (y_prompt, y_sample, nconv_p[None], nconv_s[None], npool_p[None], npool_s[None])
```

```python
import functools

import jax
import jax.numpy as jnp
from jax import lax
from jax.experimental import pallas as pl
from jax.experimental.pallas import tpu as pltpu

D_MODEL = 1024
EXPAND_WIDTH = 2048
CONV_WIDTH = 3
POOL_WINDOWS = (2, 4, 8, 16)
N_POOL_GROUPS = len(POOL_WINDOWS)
POOL_GROUP_WIDTH = EXPAND_WIDTH // N_POOL_GROUPS
POOL_HIST = max(POOL_WINDOWS) - 1
PAST_LEN = 16384
RMS_EPS = 1e-6

LANES = 128
SUBLANES = 8
CONV_CARRY_ROWS = SUBLANES
POOL_CARRY_ROWS = 2 * SUBLANES
CHUNK = POOL_GROUP_WIDTH
N_CHUNKS = EXPAND_WIDTH // CHUNK
PROMPT_TILE = 512
SAMPLE_SPLIT = 2
VMEM_PHYSICAL_BYTES = 64 * 1024 * 1024

BF16 = jnp.bfloat16
F32 = jnp.float32


def _dot(a, b):
    return jnp.dot(a, b, preferred_element_type=F32)


def _rmsnorm(x, g):
    r = lax.rsqrt(jnp.mean(x * x, axis=-1, keepdims=True) + RMS_EPS)
    return x * r * g


def _silu(z):
    return z / (1.0 + jnp.exp(-z))


def _chunk(j, section=0):
    lo = section * EXPAND_WIDTH + j * CHUNK
    return slice(lo, lo + CHUNK)


def _time_major(ref, nt):
    return jnp.concatenate([ref[:, t, :] for t in range(nt)], axis=0)


def _conv_gates(hn, win_ref, j, stage=None):
    if stage is None:
        gb, gc, v, z = (_dot(hn, win_ref[:, _chunk(j, sec)]) for sec in range(4))
    else:
        raw_refs, slot = stage
        raw_ref = raw_refs[j % len(raw_refs)]
        for sec in range(4):
            raw_ref[slot, sec] = _dot(hn, win_ref[:, _chunk(j, sec)])
        gb, gc, v, z = (raw_ref[slot, sec] for sec in range(4))
    return gb, gc * v, z


def _conv_prompt_tile(i, tm, x_ref, g_ref, win_ref, cw_ref, cb_ref, wout_ref,
                      x1_ref, nconv_ref, cvh_ref, ybuf_ref, stage, last_tile):
    @pl.when(i == 0)
    def _():
        cvh_ref[...] = jnp.zeros_like(cvh_ref)

    x = x_ref[...]
    hn = _rmsnorm(x, g_ref[0:1, :]).astype(BF16)
    for j in range(N_CHUNKS):
        c = _chunk(j)
        gb, cv, z = _conv_gates(hn, win_ref, j, stage)
        ext =jnp.concatenate([cvh_ref[:, c], cv], axis=0)
        cm1 = pltpu.roll(ext, 1, 0)[CONV_CARRY_ROWS:]
        cm2 = pltpu.roll(ext, 2, 0)[CONV_CARRY_ROWS:]
        conv = cb_ref[:, c] + cm2 * cw_ref[0:1, c]
        conv = conv + cm1 * cw_ref[1:2, c]
        conv = conv + cv * cw_ref[2:3, c]
        ybuf_ref[0:tm, c] = (gb * conv * _silu(z)).astype(BF16)
        cvh_ref[:, c] = cv[tm - CONV_CARRY_ROWS:]
    x1_ref[...] = x + _dot(ybuf_ref[0:tm, :], wout_ref[...])

    @pl.when(last_tile)
    def _():
        nconv_ref[...] = cvh_ref[CONV_CARRY_ROWS - (CONV_WIDTH - 1):, :]


def _conv_sample_step(nb, nt, xs_ref, sc_ref, g_ref, win_ref, cw_ref, cb_ref, wout_ref,
                      x1s_ref, ncs_ref, ybuf_ref):
    m = nt * nb
    xs = _time_major(xs_ref, nt)
    hn = _rmsnorm(xs, g_ref[0:1, :]).astype(BF16)
    for j in range(N_CHUNKS):
        c = _chunk(j)
        gb, cv, z = _conv_gates(hn, win_ref, j)
        full = jnp.concatenate([sc_ref[:, k, c] for k in range(CONV_WIDTH - 1)] + [cv], axis=0)
        conv = cb_ref[:, c]
        for k in range(CONV_WIDTH):
            conv = conv + full[k * nb:(k + nt) * nb] * cw_ref[k:k + 1, c]
        ybuf_ref[0:m, c] = (gb * conv * _silu(z)).astype(BF16)
        for k in range(CONV_WIDTH - 1):
            ncs_ref[:, k, c] = full[(nt + k) * nb:(nt + k + 1) * nb]
    x1 = xs + _dot(ybuf_ref[0:m, :], wout_ref[...])
    for t in range(nt):
        x1s_ref[:, t, :] = x1[t * nb:(t + 1) * nb]


def _conv_layer_kernel(xp_ref, xs_ref, sc_ref, g_ref, win_ref, cw_ref, cb_ref, wout_ref, slot_ref,
                       x1p_ref, x1s_ref, ncp_ref, ncs_ref, cvh_ref, ybuf_ref, rawa_ref, rawb_ref,
                       *, tm, tiles_per_seq, n_prompt_steps, nb, nt):
    s = pl.program_id(0)
    stage = ((rawa_ref, rawb_ref), slot_ref[0])

    @pl.when(s < n_prompt_steps)
    def _():
        i = s % tiles_per_seq
        _conv_prompt_tile(i, tm, xp_ref, g_ref, win_ref, cw_ref, cb_ref, wout_ref,
                          x1p_ref, ncp_ref, cvh_ref, ybuf_ref, stage, i == tiles_per_seq - 1)

    @pl.when(s >= n_prompt_steps)
    def _():
        _conv_sample_step(nb, nt, xs_ref, sc_ref, g_ref, win_ref, cw_ref, cb_ref, wout_ref,
                          x1s_ref, ncs_ref, ybuf_ref)


def _pool_prompt_tile(i, tm, x1_ref, g_ref, gf_ref, win_ref, wgrp_ref, ps_ref, wout_ref,
                      y_ref, npool_ref, uh_ref, ybuf_ref, stage, last_tile):
    @pl.when(i == 0)
    def _():
        uh_ref[...] = jnp.zeros_like(uh_ref)

    x1 = x1_ref[...]
    hn1 = _rmsnorm(x1, g_ref[1:2, :]).astype(BF16)
    seen = (i * tm + 1 + lax.broadcasted_iota(jnp.int32, (tm, LANES), 0)).astype(F32)
    raw_refs, slot = stage
    for g, w in enumerate(POOL_WINDOWS):
        c = _chunk(g)
        raw_ref = raw_refs[g % len(raw_refs)]
        raw_ref[slot, 0] = _dot(hn1, win_ref[:, _chunk(g, 0)])
        raw_ref[slot, 1] = _dot(hn1, win_ref[:, _chunk(g, 1)])
        u = raw_ref[slot, 0]
        s = jnp.concatenate([uh_ref[:, c], u], axis=0)
        shift = 1
        while shift < w:
            s = s + pltpu.roll(s, shift, 0)
            shift *= 2
        inv = 1.0 / jnp.minimum(jnp.float32(w), seen)
        inv = jnp.concatenate([inv] * (CHUNK // LANES), axis=1)
        p = s[POOL_CARRY_ROWS:] * inv - u
        raw_ref[slot, 2] = _dot(p.astype(BF16), wgrp_ref[g])
        q, z = raw_ref[slot, 2], raw_ref[slot, 1]
        ybuf_ref[0:tm, c] = (q * ps_ref[:, c] * _silu(z)).astype(BF16)
        uh_ref[:, c] = u[tm - POOL_CARRY_ROWS:]
    x2 = x1 + _dot(ybuf_ref[0:tm, :], wout_ref[...])
    y_ref[...] = _rmsnorm(x2, gf_ref[...])

    @pl.when(last_tile)
    def _():
        npool_ref[...] = uh_ref[POOL_CARRY_ROWS - POOL_HIST:, :]


def _pool_sample_step(g, nb, nt, start_pos, x1s_ref, sp_ref, g_ref, gf_ref, win_ref, wgrp_ref, ps_ref,
                      wout_ref, ys_ref, nps_ref, acc_ref, hn_ref):
    w = POOL_WINDOWS[g]
    c = _chunk(g)
    if g == 0:
        x1 = _time_major(x1s_ref, nt)
        acc_ref[...] = x1
        hn_ref[...] = _rmsnorm(x1, g_ref[1:2, :]).astype(BF16)
    hn1 = hn_ref[...]
    u = _dot(hn1, win_ref[:, _chunk(g, 0)])
    z = _dot(hn1, win_ref[:, _chunk(g, 1)])

    def full(r):
        if r < POOL_HIST:
            return sp_ref[:, r, :]
        return u[(r - POOL_HIST) * nb:(r - POOL_HIST + 1) * nb]

    parts = []
    for t in range(nt):
        win = full(POOL_HIST + t)
        for k in range(1, w):
            win = win + full(POOL_HIST + t - k)
        inv = 1.0 / float(min(w, start_pos + t + 1))
        parts.append(win * inv - full(POOL_HIST + t))
    p = jnp.concatenate(parts, axis=0).astype(BF16)
    q = _dot(p, wgrp_ref[g])
    y = (q * ps_ref[:, c] * _silu(z)).astype(BF16)
    acc_ref[...] += _dot(y, wout_ref[c, :])

    for r in range(POOL_HIST):
        nps_ref[:, r, :] = full(r + nt)
    if g == N_POOL_GROUPS - 1:
        ys = _rmsnorm(acc_ref[...], gf_ref[...])
        for t in range(nt):
            ys_ref[:, t, :] = ys[t * nb:(t + 1) * nb]


def _pool_layer_kernel(x1p_ref, x1s_ref, sp_ref, g_ref, gf_ref, win_ref, wgrp_ref, ps_ref, wout_ref, slot_ref,
                       yp_ref, ys_ref, npp_ref, nps_ref, uh_ref, ybuf_ref, acc_ref, hn_ref,
                       rawa_ref, rawb_ref,
                       *, tm, tiles_per_seq, n_prompt_steps, nb, nt, start_pos):
    s = pl.program_id(0)
    stage = ((rawa_ref, rawb_ref), slot_ref[0])
    phase = jnp.maximum(s - n_prompt_steps, 0) % N_POOL_GROUPS

    @pl.when(s < n_prompt_steps)
    def _():
        i = s % tiles_per_seq
        _pool_prompt_tile(i, tm, x1p_ref, g_ref, gf_ref, win_ref, wgrp_ref, ps_ref, wout_ref,
                          yp_ref, npp_ref, uh_ref, ybuf_ref, stage, i == tiles_per_seq - 1)

    for g in range(N_POOL_GROUPS):
        @pl.when(jnp.logical_and(s >= n_prompt_steps, phase == g))
        def _(g=g):
            _pool_sample_step(g, nb, nt, start_pos, x1s_ref, sp_ref, g_ref, gf_ref, win_ref, wgrp_ref,
                              ps_ref, wout_ref, ys_ref, nps_ref, acc_ref, hn_ref)


def _resident(shape):
    zeros = (0,) * len(shape)
    return pl.BlockSpec(shape, lambda s: zeros, pipeline_mode=pl.Buffered(1))


def _layer_calls(xp, xs, sc, sp, g, gf, win0, cw, cb, wout0, win1, wgrp, ps, wout1):
    B, S, D = xp.shape
    NB, NT, _ = xs.shape
    E = EXPAND_WIDTH
    tm = PROMPT_TILE
    assert S % tm == 0 and tm >= POOL_CARRY_ROWS and NB % SAMPLE_SPLIT == 0
    tiles = S // tm
    n_prompt = B * tiles
    nb = NB // SAMPLE_SPLIT
    m = NT * nb
    assert nb % SUBLANES == 0 and m <= tm

    def prompt_tile(s):
        sc_ = jnp.minimum(s, n_prompt - 1)
        return (sc_ // tiles, sc_ % tiles, 0)

    def prompt_seq(s):
        return (jnp.minimum(s, n_prompt - 1) // tiles, 0, 0)

    def conv_part(s):
        return (jnp.maximum(s - n_prompt, 0), 0, 0)

    def pool_part(s):
        return (jnp.maximum(s - n_prompt, 0) // N_POOL_GROUPS, 0, 0)

    def pool_group(s):
        q = jnp.maximum(s - n_prompt, 0)
        return (q // N_POOL_GROUPS, 0, q % N_POOL_GROUPS)

    tile_bytes = 4 * tm * D * 4
    temp_bytes = 16 * tm * CHUNK * 4 + tm * E * 2
    stage_bytes = 2 * 4 * tm * CHUNK * 4
    headroom = 4 << 20
    stage_slot = jnp.zeros((1,), jnp.int32)

    conv_sample_bytes = 2 * 4 * (2 * NT * nb * D + 2 * (CONV_WIDTH - 1) * nb * E)
    conv_limit = min(2 * (win0.size + wout0.size) + tile_bytes + temp_bytes + stage_bytes + conv_sample_bytes
                     + headroom, VMEM_PHYSICAL_BYTES - headroom)
    x1p, x1s, ncp, ncs = pl.pallas_call(
        functools.partial(_conv_layer_kernel, tm=tm, tiles_per_seq=tiles, n_prompt_steps=n_prompt,
                          nb=nb, nt=NT),
        grid=(n_prompt + SAMPLE_SPLIT,),
        in_specs=[
            pl.BlockSpec((None, tm, D), prompt_tile),
            pl.BlockSpec((nb, NT, D), conv_part),
            pl.BlockSpec((nb, CONV_WIDTH - 1, E), conv_part),
            _resident(g.shape), _resident(win0.shape), _resident(cw.shape), _resident(cb.shape),
            _resident(wout0.shape),
            pl.BlockSpec(memory_space=pltpu.SMEM),
        ],
        out_specs=[
            pl.BlockSpec((None, tm, D), prompt_tile),
            pl.BlockSpec((nb, NT, D), conv_part),
            pl.BlockSpec((None, CONV_WIDTH - 1, E), prompt_seq),
            pl.BlockSpec((nb, CONV_WIDTH - 1, E), conv_part),
        ],
        out_shape=[
            jax.ShapeDtypeStruct((B, S, D), F32),
            jax.ShapeDtypeStruct((NB, NT, D), F32),
            jax.ShapeDtypeStruct((B, CONV_WIDTH - 1, E), F32),
            jax.ShapeDtypeStruct((NB, CONV_WIDTH - 1, E), F32),
        ],
        scratch_shapes=[
            pltpu.VMEM((CONV_CARRY_ROWS, E), F32),
            pltpu.VMEM((tm, E), BF16),
            pltpu.VMEM((1, 4, tm, CHUNK), F32),
            pltpu.VMEM((1, 4, tm, CHUNK), F32),
        ],
        compiler_params=pltpu.CompilerParams(
            dimension_semantics=("arbitrary",), vmem_limit_bytes=conv_limit),
        name="conv_layer",
    )(xp, xs, sc, g, win0, cw, cb, wout0, stage_slot)

    pool_sample_bytes = 2 * 4 * (2 * NT * nb * D + 2 * nb * (POOL_HIST + 1) * CHUNK) + m * D * 6
    pool_limit = min(2 * (win1.size + wgrp.size + wout1.size) + tile_bytes + temp_bytes + stage_bytes
                     + pool_sample_bytes + headroom, VMEM_PHYSICAL_BYTES - headroom)
    yp, ys, npp, nps = pl.pallas_call(
        functools.partial(_pool_layer_kernel, tm=tm, tiles_per_seq=tiles, n_prompt_steps=n_prompt,
                          nb=nb, nt=NT, start_pos=PAST_LEN),
        grid=(n_prompt + SAMPLE_SPLIT * N_POOL_GROUPS,),
        in_specs=[
            pl.BlockSpec((None, tm, D), prompt_tile),
            pl.BlockSpec((nb, NT, D), pool_part),
            pl.BlockSpec((nb, POOL_HIST, CHUNK), pool_group),
            _resident(g.shape), _resident(gf.shape), _resident(win1.shape), _resident(wgrp.shape),
            _resident(ps.shape), _resident(wout1.shape),
            pl.BlockSpec(memory_space=pltpu.SMEM),
        ],
        out_specs=[
            pl.BlockSpec((None, tm, D), prompt_tile),
            pl.BlockSpec((nb, NT, D), pool_part),
            pl.BlockSpec((None, POOL_HIST, E), prompt_seq),
            pl.BlockSpec((nb, POOL_HIST, CHUNK), pool_group),
        ],
        out_shape=[
            jax.ShapeDtypeStruct((B, S, D), F32),
            jax.ShapeDtypeStruct((NB, NT, D), F32),
            jax.ShapeDtypeStruct((B, POOL_HIST, E), F32),
            jax.ShapeDtypeStruct((NB, POOL_HIST, E), F32),
        ],
        scratch_shapes=[
            pltpu.VMEM((POOL_CARRY_ROWS, E), F32),
            pltpu.VMEM((tm, E), BF16),
            pltpu.VMEM((m, D), F32),
            pltpu.VMEM((m, D), BF16),
            pltpu.VMEM((1, 3, tm, CHUNK), F32),
            pltpu.VMEM((1, 3, tm, CHUNK), F32),
        ],
        compiler_params=pltpu.CompilerParams(
            dimension_semantics=("arbitrary",), vmem_limit_bytes=pool_limit),
        name="pool_layer",
    )(x1p, x1s, sp, g, gf, win1, wgrp, ps, wout1, stage_slot)
    return yp, ys, ncp, ncs, npp, nps


def kernel(x_prompt, x_sample, state_conv, state_pool, norm_g, final_norm_g,
           conv_w_in, conv_w, conv_b, conv_w_out,
           pool_w_in, pool_w_grp, pool_scale, pool_w_out):
    assert norm_g.shape[0] == 2 and conv_w_in.shape[0] == 1 and pool_w_in.shape[0] == 1
    gf = final_norm_g.reshape(1, D_MODEL)
    win0 = conv_w_in[0].astype(BF16)
    wout0 = conv_w_out[0].astype(BF16)
    win1 = pool_w_in[0].astype(BF16)
    wgrp = pool_w_grp[0].astype(BF16)
    wout1 = pool_w_out[0].astype(BF16)
    yp, ys, ncp, ncs, npp, nps = _layer_calls(
        x_prompt, x_sample, state_conv[0], state_pool[0], norm_g, gf,
        win0, conv_w[0], conv_b, wout0, win1, wgrp, pool_scale, wout1)
    return (yp, ys, ncp[None], ncs[None], npp[None], nps[None])
```

```python
import functools

import jax
import jax.numpy as jnp
from jax import lax
from jax.experimental import pallas as pl
from jax.experimental.pallas import tpu as pltpu

D_MODEL = 1024
EXPAND_WIDTH = 2048
CONV_WIDTH = 3
POOL_WINDOWS = (2, 4, 8, 16)
N_POOL_GROUPS = len(POOL_WINDOWS)
POOL_GROUP_WIDTH = EXPAND_WIDTH // N_POOL_GROUPS
POOL_HIST = max(POOL_WINDOWS) - 1
PAST_LEN = 16384
RMS_EPS = 1e-6

LANES = 128
SUBLANES = 8
CONV_CARRY_ROWS = SUBLANES
POOL_CARRY_ROWS = 2 * SUBLANES
CHUNK = POOL_GROUP_WIDTH
N_CHUNKS = EXPAND_WIDTH // CHUNK
PROMPT_TILE = 512
SAMPLE_SPLIT = 2
VMEM_PHYSICAL_BYTES = 64 * 1024 * 1024

BF16 = jnp.bfloat16
F32 = jnp.float32


def _dot(a, b):
    return jnp.dot(a, b, preferred_element_type=F32)


def _rmsnorm(x, g):
    r = lax.rsqrt(jnp.mean(x * x, axis=-1, keepdims=True) + RMS_EPS)
    return x * r * g


def _silu(z):
    return z / (1.0 + jnp.exp(-z))


def _chunk(j, section=0):
    lo = section * EXPAND_WIDTH + j * CHUNK
    return slice(lo, lo + CHUNK)


def _time_major(ref, nt):
    return jnp.concatenate([ref[:, t, :] for t in range(nt)], axis=0)


def _conv_gates(hn, win_ref, j, stage=None):
    if stage is None:
        gb, gc, v, z = (_dot(hn, win_ref[:, _chunk(j, sec)]) for sec in range(4))
    else:
        raw_refs, slot = stage
        raw_ref = raw_refs[j % len(raw_refs)]
        for sec in range(4):
            raw_ref[slot, sec] = _dot(hn, win_ref[:, _chunk(j, sec)])
        gb, gc, v, z = (raw_ref[slot, sec] for sec in range(4))
    return gb, gc * v, z


def _conv_prompt_tile(i, tm, x_ref, g_ref, win_ref, cw_ref, cb_ref, wout_ref,
                      x1_ref, nconv_ref, cvh_ref, ybuf_ref, stage, last_tile):
    @pl.when(i == 0)
    def _():
        cvh_ref[...] = jnp.zeros_like(cvh_ref)

    x = x_ref[...]
    hn = _rmsnorm(x, g_ref[0:1, :]).astype(BF16)
    for j in range(N_CHUNKS):
        c = _chunk(j)
        gb, cv, z = _conv_gates(hn, win_ref, j, stage)
        ext =jnp.concatenate([cvh_ref[:, c], cv], axis=0)
        cm1 = pltpu.roll(ext, 1, 0)[CONV_CARRY_ROWS:]
        cm2 = pltpu.roll(ext, 2, 0)[CONV_CARRY_ROWS:]
        conv = cb_ref[:, c] + cm2 * cw_ref[0:1, c]
        conv = conv + cm1 * cw_ref[1:2, c]
        conv = conv + cv * cw_ref[2:3, c]
        ybuf_ref[0:tm, c] = (gb * conv * _silu(z)).astype(BF16)
        cvh_ref[:, c] = cv[tm - CONV_CARRY_ROWS:]
    x1_ref[...] = x + _dot(ybuf_ref[0:tm, :], wout_ref[...])

    @pl.when(last_tile)
    def _():
        nconv_ref[...] = cvh_ref[CONV_CARRY_ROWS - (CONV_WIDTH - 1):, :]


def _conv_sample_step(nb, nt, xs_ref, sc_ref, g_ref, win_ref, cw_ref, cb_ref, wout_ref,
                      x1s_ref, ncs_ref, ybuf_ref):
    m = nt * nb
    xs = _time_major(xs_ref, nt)
    hn = _rmsnorm(xs, g_ref[0:1, :]).astype(BF16)
    for j in range(N_CHUNKS):
        c = _chunk(j)
        gb, cv, z = _conv_gates(hn, win_ref, j)
        full = jnp.concatenate([sc_ref[:, k, c] for k in range(CONV_WIDTH - 1)] + [cv], axis=0)
        conv = cb_ref[:, c]
        for k in range(CONV_WIDTH):
            conv = conv + full[k * nb:(k + nt) * nb] * cw_ref[k:k + 1, c]
        ybuf_ref[0:m, c] = (gb * conv * _silu(z)).astype(BF16)
        for k in range(CONV_WIDTH - 1):
            ncs_ref[:, k, c] = full[(nt + k) * nb:(nt + k + 1) * nb]
    x1 = xs + _dot(ybuf_ref[0:m, :], wout_ref[...])
    for t in range(nt):
        x1s_ref[:, t, :] = x1[t * nb:(t + 1) * nb]


def _conv_layer_kernel(xp_ref, xs_ref, sc_ref, g_ref, win_ref, cw_ref, cb_ref, wout_ref, slot_ref,
                       x1p_ref, x1s_ref, ncp_ref, ncs_ref, cvh_ref, ybuf_ref, rawa_ref, rawb_ref,
                       *, tm, tiles_per_seq, n_prompt_steps, nb, nt):
    s = pl.program_id(0)
    stage = ((rawa_ref, rawb_ref), slot_ref[0])

    @pl.when(s < n_prompt_steps)
    def _():
        i = s % tiles_per_seq
        _conv_prompt_tile(i, tm, xp_ref, g_ref, win_ref, cw_ref, cb_ref, wout_ref,
                          x1p_ref, ncp_ref, cvh_ref, ybuf_ref, stage, i == tiles_per_seq - 1)

    @pl.when(s >= n_prompt_steps)
    def _():
        _conv_sample_step(nb, nt, xs_ref, sc_ref, g_ref, win_ref, cw_ref, cb_ref, wout_ref,
                          x1s_ref, ncs_ref, ybuf_ref)


def _pool_prompt_tile(i, tm, x1_ref, g_ref, gf_ref, win_ref, wgrp_ref, ps_ref, wout_ref,
                      y_ref, npool_ref, uh_ref, ybuf_ref, stage, last_tile):
    @pl.when(i == 0)
    def _():
        uh_ref[...] = jnp.zeros_like(uh_ref)

    x1 = x1_ref[...]
    hn1 = _rmsnorm(x1, g_ref[1:2, :]).astype(BF16)
    seen = (i * tm + 1 + lax.broadcasted_iota(jnp.int32, (tm, LANES), 0)).astype(F32)
    raw_refs, slot = stage
    for g, w in enumerate(POOL_WINDOWS):
        c = _chunk(g)
        raw_ref = raw_refs[g % len(raw_refs)]
        raw_ref[slot, 0] = _dot(hn1, win_ref[:, _chunk(g, 0)])
        raw_ref[slot, 1] = _dot(hn1, win_ref[:, _chunk(g, 1)])
        u = raw_ref[slot, 0]
        s = jnp.concatenate([uh_ref[:, c], u], axis=0)
        shift = 1
        while shift < w:
            s = s + pltpu.roll(s, shift, 0)
            shift *= 2
        inv = 1.0 / jnp.minimum(jnp.float32(w), seen)
        inv = jnp.concatenate([inv] * (CHUNK // LANES), axis=1)
        p = s[POOL_CARRY_ROWS:] * inv - u
        raw_ref[slot, 2] = _dot(p.astype(BF16), wgrp_ref[g])
        q, z = raw_ref[slot, 2], raw_ref[slot, 1]
        ybuf_ref[0:tm, c] = (q * ps_ref[:, c] * _silu(z)).astype(BF16)
        uh_ref[:, c] = u[tm - POOL_CARRY_ROWS:]
    x2 = x1 + _dot(ybuf_ref[0:tm, :], wout_ref[...])
    y_ref[...] = _rmsnorm(x2, gf_ref[...])

    @pl.when(last_tile)
    def _():
        npool_ref[...] = uh_ref[POOL_CARRY_ROWS - POOL_HIST:, :]


def _pool_sample_step(g, nb, nt, start_pos, x1s_ref, sp_ref, g_ref, gf_ref, win_ref, wgrp_ref, ps_ref,
                      wout_ref, ys_ref, nps_ref, acc_ref, hn_ref):
    w = POOL_WINDOWS[g]
    c = _chunk(g)
    if g == 0:
        x1 = _time_major(x1s_ref, nt)
        acc_ref[...] = x1
        hn_ref[...] = _rmsnorm(x1, g_ref[1:2, :]).astype(BF16)
    hn1 = hn_ref[...]
    u = _dot(hn1, win_ref[:, _chunk(g, 0)])
    z = _dot(hn1, win_ref[:, _chunk(g, 1)])

    def full(r):
        if r < POOL_HIST:
            return sp_ref[r]
        return u[(r - POOL_HIST) * nb:(r - POOL_HIST + 1) * nb]

    parts = []
    for t in range(nt):
        win = full(POOL_HIST + t)
        for k in range(1, w):
            win = win + full(POOL_HIST + t - k)
        inv = 1.0 / float(min(w, start_pos + t + 1))
        parts.append(win * inv - full(POOL_HIST + t))
    p = jnp.concatenate(parts, axis=0).astype(BF16)
    q = _dot(p, wgrp_ref[g])
    y = (q * ps_ref[:, c] * _silu(z)).astype(BF16)
    acc_ref[...] += _dot(y, wout_ref[c, :])

    for r in range(POOL_HIST):
        nps_ref[r] = full(r + nt)
    if g == N_POOL_GROUPS - 1:
        ys = _rmsnorm(acc_ref[...], gf_ref[...])
        for t in range(nt):
            ys_ref[:, t, :] = ys[t * nb:(t + 1) * nb]


def _pool_layer_kernel(x1p_ref, x1s_ref, sp_ref, g_ref, gf_ref, win_ref, wgrp_ref, ps_ref, wout_ref, slot_ref,
                       yp_ref, ys_ref, npp_ref, nps_ref, uh_ref, ybuf_ref, acc_ref, hn_ref,
                       rawa_ref, rawb_ref,
                       *, tm, tiles_per_seq, n_prompt_steps, nb, nt, start_pos):
    s = pl.program_id(0)
    stage = ((rawa_ref, rawb_ref), slot_ref[0])
    phase = jnp.maximum(s - n_prompt_steps, 0) % N_POOL_GROUPS

    @pl.when(s < n_prompt_steps)
    def _():
        i = s % tiles_per_seq
        _pool_prompt_tile(i, tm, x1p_ref, g_ref, gf_ref, win_ref, wgrp_ref, ps_ref, wout_ref,
                          yp_ref, npp_ref, uh_ref, ybuf_ref, stage, i == tiles_per_seq - 1)

    for g in range(N_POOL_GROUPS):
        @pl.when(jnp.logical_and(s >= n_prompt_steps, phase == g))
        def _(g=g):
            _pool_sample_step(g, nb, nt, start_pos, x1s_ref, sp_ref, g_ref, gf_ref, win_ref, wgrp_ref,
                              ps_ref, wout_ref, ys_ref, nps_ref, acc_ref, hn_ref)


def _resident(shape):
    zeros = (0,) * len(shape)
    return pl.BlockSpec(shape, lambda s: zeros, pipeline_mode=pl.Buffered(1))


def _layer_calls(xp, xs, sc, sp, g, gf, win0, cw, cb, wout0, win1, wgrp, ps, wout1):
    B, S, D = xp.shape
    NB, NT, _ = xs.shape
    E = EXPAND_WIDTH
    tm = PROMPT_TILE
    assert S % tm == 0 and tm >= POOL_CARRY_ROWS and NB % SAMPLE_SPLIT == 0
    tiles = S // tm
    n_prompt = B * tiles
    nb = NB // SAMPLE_SPLIT
    m = NT * nb
    assert nb % SUBLANES == 0 and m <= tm

    def prompt_tile(s):
        sc_ = jnp.minimum(s, n_prompt - 1)
        return (sc_ // tiles, sc_ % tiles, 0)

    def prompt_seq(s):
        return (jnp.minimum(s, n_prompt - 1) // tiles, 0, 0)

    def conv_part(s):
        return (jnp.maximum(s - n_prompt, 0), 0, 0)

    def pool_part(s):
        return (jnp.maximum(s - n_prompt, 0) // N_POOL_GROUPS, 0, 0)

    def pool_group(s):
        q = jnp.maximum(s - n_prompt, 0)
        return (0, q // N_POOL_GROUPS, q % N_POOL_GROUPS)

    tile_bytes = 4 * tm * D * 4
    temp_bytes = 16 * tm * CHUNK * 4 + tm * E * 2
    stage_bytes = 2 * 4 * tm * CHUNK * 4
    headroom = 4 << 20
    stage_slot = jnp.zeros((1,), jnp.int32)

    conv_sample_bytes = 2 * 4 * (2 * NT * nb * D + 2 * (CONV_WIDTH - 1) * nb * E)
    conv_limit = min(2 * (win0.size + wout0.size) + tile_bytes + temp_bytes + stage_bytes + conv_sample_bytes
                     + headroom, VMEM_PHYSICAL_BYTES - headroom)
    x1p, x1s, ncp, ncs = pl.pallas_call(
        functools.partial(_conv_layer_kernel, tm=tm, tiles_per_seq=tiles, n_prompt_steps=n_prompt,
                          nb=nb, nt=NT),
        grid=(n_prompt + SAMPLE_SPLIT,),
        in_specs=[
            pl.BlockSpec((None, tm, D), prompt_tile),
            pl.BlockSpec((nb, NT, D), conv_part),
            pl.BlockSpec((nb, CONV_WIDTH - 1, E), conv_part),
            _resident(g.shape), _resident(win0.shape), _resident(cw.shape), _resident(cb.shape),
            _resident(wout0.shape),
            pl.BlockSpec(memory_space=pltpu.SMEM),
        ],
        out_specs=[
            pl.BlockSpec((None, tm, D), prompt_tile),
            pl.BlockSpec((nb, NT, D), conv_part),
            pl.BlockSpec((None, CONV_WIDTH - 1, E), prompt_seq),
            pl.BlockSpec((nb, CONV_WIDTH - 1, E), conv_part),
        ],
        out_shape=[
            jax.ShapeDtypeStruct((B, S, D), F32),
            jax.ShapeDtypeStruct((NB, NT, D), F32),
            jax.ShapeDtypeStruct((B, CONV_WIDTH - 1, E), F32),
            jax.ShapeDtypeStruct((NB, CONV_WIDTH - 1, E), F32),
        ],
        scratch_shapes=[
            pltpu.VMEM((CONV_CARRY_ROWS, E), F32),
            pltpu.VMEM((tm, E), BF16),
            pltpu.VMEM((1, 4, tm, CHUNK), F32),
            pltpu.VMEM((1, 4, tm, CHUNK), F32),
        ],
        compiler_params=pltpu.CompilerParams(
            dimension_semantics=("arbitrary",), vmem_limit_bytes=conv_limit),
        name="conv_layer",
    )(xp, xs, sc, g, win0, cw, cb, wout0, stage_slot)

    pool_sample_bytes = 2 * 4 * (2 * NT * nb * D + 2 * nb * (POOL_HIST + 1) * CHUNK) + m * D * 6
    pool_limit = min(2 * (win1.size + wgrp.size + wout1.size) + tile_bytes + temp_bytes + stage_bytes
                     + pool_sample_bytes + headroom, VMEM_PHYSICAL_BYTES - headroom)
    yp, ys, npp, nps = pl.pallas_call(
        functools.partial(_pool_layer_kernel, tm=tm, tiles_per_seq=tiles, n_prompt_steps=n_prompt,
                          nb=nb, nt=NT, start_pos=PAST_LEN),
        grid=(n_prompt + SAMPLE_SPLIT * N_POOL_GROUPS,),
        in_specs=[
            pl.BlockSpec((None, tm, D), prompt_tile),
            pl.BlockSpec((nb, NT, D), pool_part),
            pl.BlockSpec((POOL_HIST, nb, CHUNK), pool_group),
            _resident(g.shape), _resident(gf.shape), _resident(win1.shape), _resident(wgrp.shape),
            _resident(ps.shape), _resident(wout1.shape),
            pl.BlockSpec(memory_space=pltpu.SMEM),
        ],
        out_specs=[
            pl.BlockSpec((None, tm, D), prompt_tile),
            pl.BlockSpec((nb, NT, D), pool_part),
            pl.BlockSpec((None, POOL_HIST, E), prompt_seq),
            pl.BlockSpec((POOL_HIST, nb, CHUNK), pool_group),
        ],
        out_shape=[
            jax.ShapeDtypeStruct((B, S, D), F32),
            jax.ShapeDtypeStruct((NB, NT, D), F32),
            jax.ShapeDtypeStruct((B, POOL_HIST, E), F32),
            jax.ShapeDtypeStruct((POOL_HIST, NB, E), F32),
        ],
        scratch_shapes=[
            pltpu.VMEM((POOL_CARRY_ROWS, E), F32),
            pltpu.VMEM((tm, E), BF16),
            pltpu.VMEM((m, D), F32),
            pltpu.VMEM((m, D), BF16),
            pltpu.VMEM((1, 3, tm, CHUNK), F32),
            pltpu.VMEM((1, 3, tm, CHUNK), F32),
        ],
        compiler_params=pltpu.CompilerParams(
            dimension_semantics=("arbitrary",), vmem_limit_bytes=pool_limit),
        name="pool_layer",
    )(x1p, x1s, sp, g, gf, win1, wgrp, ps, wout1, stage_slot)
    return yp, ys, ncp, ncs, npp, nps


def kernel(x_prompt, x_sample, state_conv, state_pool, norm_g, final_norm_g,
           conv_w_in, conv_w, conv_b, conv_w_out,
           pool_w_in, pool_w_grp, pool_scale, pool_w_out):
    assert norm_g.shape[0] == 2 and conv_w_in.shape[0] == 1 and pool_w_in.shape[0] == 1
    gf = final_norm_g.reshape(1, D_MODEL)
    win0 = conv_w_in[0].astype(BF16)
    wout0 = conv_w_out[0].astype(BF16)
    win1 = pool_w_in[0].astype(BF16)
    wgrp = pool_w_grp[0].astype(BF16)
    wout1 = pool_w_out[0].astype(BF16)
    yp, ys, ncp, ncs, npp, nps = _layer_calls(
        x_prompt, x_sample, state_conv[0], jnp.transpose(state_pool[0], (1, 0, 2)), norm_g, gf,
        win0, conv_w[0], conv_b, wout0, win1, wgrp, pool_scale, wout1)
    return (yp, ys, ncp[None], ncs[None], npp[None], jnp.transpose(nps, (1, 0, 2))[None])
```

```python
import functools

import jax
import jax.numpy as jnp
from jax import lax
from jax.experimental import pallas as pl
from jax.experimental.pallas import tpu as pltpu

D_MODEL = 1024
EXPAND_WIDTH = 2048
CONV_WIDTH = 3
POOL_WINDOWS = (2, 4, 8, 16)
N_POOL_GROUPS = len(POOL_WINDOWS)
POOL_GROUP_WIDTH = EXPAND_WIDTH // N_POOL_GROUPS
POOL_HIST = max(POOL_WINDOWS) - 1
PAST_LEN = 16384
RMS_EPS = 1e-6

LANES = 128
SUBLANES = 8
CONV_CARRY_ROWS = SUBLANES
POOL_CARRY_ROWS = 2 * SUBLANES
CHUNK = POOL_GROUP_WIDTH
N_CHUNKS = EXPAND_WIDTH // CHUNK
PROMPT_TILE = 512
SAMPLE_SPLIT = 2
WEIGHT_PIECE = (1024, 512)
WEIGHT_STAGE_SLOTS = 3
VMEM_PHYSICAL_BYTES = 64 * 1024 * 1024

BF16 = jnp.bfloat16
F32 = jnp.float32


def _dot(a, b):
    return jnp.dot(a, b, preferred_element_type=F32)


def _rmsnorm(x, g):
    r = lax.rsqrt(jnp.mean(x * x, axis=-1, keepdims=True) + RMS_EPS)
    return x * r * g


def _silu(z):
    return z / (1.0 + jnp.exp(-z))


def _chunk(j, section=0):
    lo = section * EXPAND_WIDTH + j * CHUNK
    return slice(lo, lo + CHUNK)


def _weight_pieces(src_hbm, dst_ref):
    pr, pc = WEIGHT_PIECE
    rows, cols = src_hbm.shape
    assert rows % pr == 0 and cols % pc == 0
    pairs = []
    for i in range(rows // pr):
        for j in range(cols // pc):
            r, c = pl.ds(i * pr, pr), pl.ds(j * pc, pc)
            pairs.append((src_hbm.at[r, c], dst_ref.at[r, c]))
    return pairs


def _load_weights_as_bf16(pieces, stage_ref, sem_ref):
    n, slots = len(pieces), stage_ref.shape[0]

    def copy(k):
        return pltpu.make_async_copy(pieces[k][0], stage_ref.at[k % slots], sem_ref.at[k % slots])

    for k in range(min(slots, n)):
        copy(k).start()
    for k in range(n):
        copy(k).wait()
        pieces[k][1][...] = stage_ref[k % slots].astype(BF16)
        if k + slots < n:
            copy(k + slots).start()


def _time_major(ref, nt):
    return jnp.concatenate([ref[:, t, :] for t in range(nt)], axis=0)


def _conv_gates(hn, win_ref, j, stage=None):
    if stage is None:
        gb, gc, v, z = (_dot(hn, win_ref[:, _chunk(j, sec)]) for sec in range(4))
    else:
        raw_refs, slot = stage
        raw_ref = raw_refs[j % len(raw_refs)]
        for sec in range(4):
            raw_ref[slot, sec] = _dot(hn, win_ref[:, _chunk(j, sec)])
        gb, gc, v, z = (raw_ref[slot, sec] for sec in range(4))
    return gb, gc * v, z


def _conv_prompt_tile(i, tm, x_ref, g_ref, win_ref, cw_ref, cb_ref, wout_ref,
                      x1_ref, nconv_ref, cvh_ref, ybuf_ref, stage, last_tile):
    @pl.when(i == 0)
    def _():
        cvh_ref[...] = jnp.zeros_like(cvh_ref)

    x = x_ref[...]
    hn = _rmsnorm(x, g_ref[0:1, :]).astype(BF16)
    for j in range(N_CHUNKS):
        c = _chunk(j)
        gb, cv, z = _conv_gates(hn, win_ref, j, stage)
        ext = jnp.concatenate([cvh_ref[:, c], cv], axis=0)
        cm1 = pltpu.roll(ext, 1, 0)[CONV_CARRY_ROWS:]
        cm2 = pltpu.roll(ext, 2, 0)[CONV_CARRY_ROWS:]
        conv = cb_ref[:, c] + cm2 * cw_ref[0:1, c]
        conv = conv + cm1 * cw_ref[1:2, c]
        conv = conv + cv * cw_ref[2:3, c]
        ybuf_ref[0:tm, c] = (gb * conv * _silu(z)).astype(BF16)
        cvh_ref[:, c] = cv[tm - CONV_CARRY_ROWS:]
    x1_ref[...] = x + _dot(ybuf_ref[0:tm, :], wout_ref[...])

    @pl.when(last_tile)
    def _():
        nconv_ref[...] = cvh_ref[CONV_CARRY_ROWS - (CONV_WIDTH - 1):, :]


def _conv_sample_step(nb, nt, xs_ref, sc_ref, g_ref, win_ref, cw_ref, cb_ref, wout_ref,
                      x1s_ref, ncs_ref, ybuf_ref):
    m = nt * nb
    xs = _time_major(xs_ref, nt)
    hn = _rmsnorm(xs, g_ref[0:1, :]).astype(BF16)
    for j in range(N_CHUNKS):
        c = _chunk(j)
        gb, cv, z = _conv_gates(hn, win_ref, j)
        full = jnp.concatenate([sc_ref[:, k, c] for k in range(CONV_WIDTH - 1)] + [cv], axis=0)
        conv = cb_ref[:, c]
        for k in range(CONV_WIDTH):
            conv = conv + full[k * nb:(k + nt) * nb] * cw_ref[k:k + 1, c]
        ybuf_ref[0:m, c] = (gb * conv * _silu(z)).astype(BF16)
        for k in range(CONV_WIDTH - 1):
            ncs_ref[:, k, c] = full[(nt + k) * nb:(nt + k + 1) * nb]
    x1 = xs + _dot(ybuf_ref[0:m, :], wout_ref[...])
    for t in range(nt):
        x1s_ref[:, t, :] = x1[t * nb:(t + 1) * nb]


def _conv_layer_kernel(xp_ref, xs_ref, sc_ref, g_ref, win_hbm, cw_ref, cb_ref, wout_hbm, slot_ref,
                       x1p_ref, x1s_ref, ncp_ref, ncs_ref, cvh_ref, ybuf_ref, rawa_ref, rawb_ref,
                       win_ref, wout_ref, wstage_ref, wsem_ref,
                       *, tm, tiles_per_seq, n_prompt_steps, nb, nt):
    s = pl.program_id(0)
    stage = ((rawa_ref, rawb_ref), slot_ref[0])

    @pl.when(s == 0)
    def _():
        _load_weights_as_bf16(_weight_pieces(win_hbm, win_ref) + _weight_pieces(wout_hbm, wout_ref),
                              wstage_ref, wsem_ref)

    @pl.when(s < n_prompt_steps)
    def _():
        i = s % tiles_per_seq
        _conv_prompt_tile(i, tm, xp_ref, g_ref, win_ref, cw_ref, cb_ref, wout_ref,
                          x1p_ref, ncp_ref, cvh_ref, ybuf_ref, stage, i == tiles_per_seq - 1)

    @pl.when(s >= n_prompt_steps)
    def _():
        _conv_sample_step(nb, nt, xs_ref, sc_ref, g_ref, win_ref, cw_ref, cb_ref, wout_ref,
                          x1s_ref, ncs_ref, ybuf_ref)


def _pool_prompt_tile(i, tm, x1_ref, g_ref, gf_ref, win_ref, wgrp_ref, ps_ref, wout_ref,
                      y_ref, npool_ref, uh_ref, ybuf_ref, stage, last_tile):
    @pl.when(i == 0)
    def _():
        uh_ref[...] = jnp.zeros_like(uh_ref)

    x1 = x1_ref[...]
    hn1 = _rmsnorm(x1, g_ref[1:2, :]).astype(BF16)
    seen = (i * tm + 1 + lax.broadcasted_iota(jnp.int32, (tm, LANES), 0)).astype(F32)
    raw_refs, slot = stage
    for g, w in enumerate(POOL_WINDOWS):
        c = _chunk(g)
        raw_ref = raw_refs[g % len(raw_refs)]
        raw_ref[slot, 0] = _dot(hn1, win_ref[:, _chunk(g, 0)])
        raw_ref[slot, 1] = _dot(hn1, win_ref[:, _chunk(g, 1)])
        u = raw_ref[slot, 0]
        s = jnp.concatenate([uh_ref[:, c], u], axis=0)
        shift = 1
        while shift < w:
            s = s + pltpu.roll(s, shift, 0)
            shift *= 2
        inv = 1.0 / jnp.minimum(jnp.float32(w), seen)
        inv = jnp.concatenate([inv] * (CHUNK // LANES), axis=1)
        p = s[POOL_CARRY_ROWS:] * inv - u
        raw_ref[slot, 2] = _dot(p.astype(BF16), wgrp_ref[_chunk(g), :])
        q, z = raw_ref[slot, 2], raw_ref[slot, 1]
        ybuf_ref[0:tm, c] = (q * ps_ref[:, c] * _silu(z)).astype(BF16)
        uh_ref[:, c] = u[tm - POOL_CARRY_ROWS:]
    x2 = x1 + _dot(ybuf_ref[0:tm, :], wout_ref[...])
    y_ref[...] = _rmsnorm(x2, gf_ref[...])

    @pl.when(last_tile)
    def _():
        npool_ref[...] = uh_ref[POOL_CARRY_ROWS - POOL_HIST:, :]


def _pool_sample_step(g, nb, nt, start_pos, x1s_ref, sp_ref, g_ref, gf_ref, win_ref, wgrp_ref, ps_ref,
                      wout_ref, ys_ref, nps_ref, acc_ref, hn_ref):
    w = POOL_WINDOWS[g]
    c = _chunk(g)
    if g == 0:
        x1 = _time_major(x1s_ref, nt)
        acc_ref[...] = x1
        hn_ref[...] = _rmsnorm(x1, g_ref[1:2, :]).astype(BF16)
    hn1 = hn_ref[...]
    u = _dot(hn1, win_ref[:, _chunk(g, 0)])
    z = _dot(hn1, win_ref[:, _chunk(g, 1)])

    def full(r):
        if r < POOL_HIST:
            return sp_ref[r]
        return u[(r - POOL_HIST) * nb:(r - POOL_HIST + 1) * nb]

    parts = []
    for t in range(nt):
        win = full(POOL_HIST + t)
        for k in range(1, w):
            win = win + full(POOL_HIST + t - k)
        inv = 1.0 / float(min(w, start_pos + t + 1))
        parts.append(win * inv - full(POOL_HIST + t))
    p = jnp.concatenate(parts, axis=0).astype(BF16)
    q = _dot(p, wgrp_ref[_chunk(g), :])
    y = (q * ps_ref[:, c] * _silu(z)).astype(BF16)
    acc_ref[...] += _dot(y, wout_ref[c, :])

    for r in range(POOL_HIST):
        nps_ref[r] = full(r + nt)
    if g == N_POOL_GROUPS - 1:
        ys = _rmsnorm(acc_ref[...], gf_ref[...])
        for t in range(nt):
            ys_ref[:, t, :] = ys[t * nb:(t + 1) * nb]


def _pool_layer_kernel(x1p_ref, x1s_ref, sp_ref, g_ref, gf_ref, win_hbm, wgrp_hbm, ps_ref, wout_hbm, slot_ref,
                       yp_ref, ys_ref, npp_ref, nps_ref, uh_ref, ybuf_ref, acc_ref, hn_ref,
                       rawa_ref, rawb_ref, win_ref, wgrp_ref, wout_ref, wstage_ref, wsem_ref,
                       *, tm, tiles_per_seq, n_prompt_steps, nb, nt, start_pos):
    s = pl.program_id(0)
    stage = ((rawa_ref, rawb_ref), slot_ref[0])
    phase = jnp.maximum(s - n_prompt_steps, 0) % N_POOL_GROUPS

    @pl.when(s == 0)
    def _():
        _load_weights_as_bf16(_weight_pieces(win_hbm, win_ref) + _weight_pieces(wgrp_hbm, wgrp_ref)
                              + _weight_pieces(wout_hbm, wout_ref), wstage_ref, wsem_ref)

    @pl.when(s < n_prompt_steps)
    def _():
        i = s % tiles_per_seq
        _pool_prompt_tile(i, tm, x1p_ref, g_ref, gf_ref, win_ref, wgrp_ref, ps_ref, wout_ref,
                          yp_ref, npp_ref, uh_ref, ybuf_ref, stage, i == tiles_per_seq - 1)

    for g in range(N_POOL_GROUPS):
        @pl.when(jnp.logical_and(s >= n_prompt_steps, phase == g))
        def _(g=g):
            _pool_sample_step(g, nb, nt, start_pos, x1s_ref, sp_ref, g_ref, gf_ref, win_ref, wgrp_ref,
                              ps_ref, wout_ref, ys_ref, nps_ref, acc_ref, hn_ref)


def _resident(shape):
    zeros = (0,) * len(shape)
    return pl.BlockSpec(shape, lambda s: zeros, pipeline_mode=pl.Buffered(1))


def _layer_calls(xp, xs, sc, sp, g, gf, win0, cw, cb, wout0, win1, wgrp, ps, wout1):
    B, S, D = xp.shape
    NB, NT, _ = xs.shape
    E = EXPAND_WIDTH
    tm = PROMPT_TILE
    assert S % tm == 0 and tm >= POOL_CARRY_ROWS and NB % SAMPLE_SPLIT == 0
    tiles = S // tm
    n_prompt = B * tiles
    nb = NB // SAMPLE_SPLIT
    m = NT * nb
    assert nb % SUBLANES == 0 and m <= tm

    def prompt_tile(s):
        sc_ = jnp.minimum(s, n_prompt - 1)
        return (sc_ // tiles, sc_ % tiles, 0)

    def prompt_seq(s):
        return (jnp.minimum(s, n_prompt - 1) // tiles, 0, 0)

    def conv_part(s):
        return (jnp.maximum(s - n_prompt, 0), 0, 0)

    def pool_part(s):
        return (jnp.maximum(s - n_prompt, 0) // N_POOL_GROUPS, 0, 0)

    def pool_group(s):
        q = jnp.maximum(s - n_prompt, 0)
        return (0, q // N_POOL_GROUPS, q % N_POOL_GROUPS)

    tile_bytes = 4 * tm * D * 4
    temp_bytes = 16 * tm * CHUNK * 4 + tm * E * 2
    stage_bytes = 2 * 4 * tm * CHUNK * 4
    headroom = 4 << 20
    stage_slot = jnp.zeros((1,), jnp.int32)
    hbm = pl.BlockSpec(memory_space=pl.ANY)
    weight_stage = [pltpu.VMEM((WEIGHT_STAGE_SLOTS,) + WEIGHT_PIECE, F32),
                    pltpu.SemaphoreType.DMA((WEIGHT_STAGE_SLOTS,))]
    weight_stage_bytes = WEIGHT_STAGE_SLOTS * WEIGHT_PIECE[0] * WEIGHT_PIECE[1] * 4

    conv_sample_bytes = 2 * 4 * (2 * NT * nb * D + 2 * (CONV_WIDTH - 1) * nb * E)
    conv_limit = min(2 * (win0.size + wout0.size) + weight_stage_bytes + tile_bytes + temp_bytes + stage_bytes
                     + conv_sample_bytes + headroom, VMEM_PHYSICAL_BYTES - headroom)
    x1p, x1s, ncp, ncs = pl.pallas_call(
        functools.partial(_conv_layer_kernel, tm=tm, tiles_per_seq=tiles, n_prompt_steps=n_prompt,
                          nb=nb, nt=NT),
        grid=(n_prompt + SAMPLE_SPLIT,),
        in_specs=[
            pl.BlockSpec((None, tm, D), prompt_tile),
            pl.BlockSpec((nb, NT, D), conv_part),
            pl.BlockSpec((nb, CONV_WIDTH - 1, E), conv_part),
            _resident(g.shape), hbm, _resident(cw.shape), _resident(cb.shape), hbm,
            pl.BlockSpec(memory_space=pltpu.SMEM),
        ],
        out_specs=[
            pl.BlockSpec((None, tm, D), prompt_tile),
            pl.BlockSpec((nb, NT, D), conv_part),
            pl.BlockSpec((None, CONV_WIDTH - 1, E), prompt_seq),
            pl.BlockSpec((nb, CONV_WIDTH - 1, E), conv_part),
        ],
        out_shape=[
            jax.ShapeDtypeStruct((B, S, D), F32),
            jax.ShapeDtypeStruct((NB, NT, D), F32),
            jax.ShapeDtypeStruct((B, CONV_WIDTH - 1, E), F32),
            jax.ShapeDtypeStruct((NB, CONV_WIDTH - 1, E), F32),
        ],
        scratch_shapes=[
            pltpu.VMEM((CONV_CARRY_ROWS, E), F32),
            pltpu.VMEM((tm, E), BF16),
            pltpu.VMEM((1, 4, tm, CHUNK), F32),
            pltpu.VMEM((1, 4, tm, CHUNK), F32),
            pltpu.VMEM(win0.shape, BF16),
            pltpu.VMEM(wout0.shape, BF16),
        ] + weight_stage,
        compiler_params=pltpu.CompilerParams(
            dimension_semantics=("arbitrary",), vmem_limit_bytes=conv_limit),
        name="conv_layer",
    )(xp, xs, sc, g, win0, cw, cb, wout0, stage_slot)

    pool_sample_bytes = 2 * 4 * (2 * NT * nb * D + 2 * nb * (POOL_HIST + 1) * CHUNK) + m * D * 6
    pool_limit = min(2 * (win1.size + wgrp.size + wout1.size) + weight_stage_bytes + tile_bytes + temp_bytes
                     + stage_bytes + pool_sample_bytes + headroom, VMEM_PHYSICAL_BYTES - headroom)
    yp, ys, npp, nps = pl.pallas_call(
        functools.partial(_pool_layer_kernel, tm=tm, tiles_per_seq=tiles, n_prompt_steps=n_prompt,
                          nb=nb, nt=NT, start_pos=PAST_LEN),
        grid=(n_prompt + SAMPLE_SPLIT * N_POOL_GROUPS,),
        in_specs=[
            pl.BlockSpec((None, tm, D), prompt_tile),
            pl.BlockSpec((nb, NT, D), pool_part),
            pl.BlockSpec((POOL_HIST, nb, CHUNK), pool_group),
            _resident(g.shape), _resident(gf.shape), hbm, hbm, _resident(ps.shape), hbm,
            pl.BlockSpec(memory_space=pltpu.SMEM),
        ],
        out_specs=[
            pl.BlockSpec((None, tm, D), prompt_tile),
            pl.BlockSpec((nb, NT, D), pool_part),
            pl.BlockSpec((None, POOL_HIST, E), prompt_seq),
            pl.BlockSpec((POOL_HIST, nb, CHUNK), pool_group),
        ],
        out_shape=[
            jax.ShapeDtypeStruct((B, S, D), F32),
            jax.ShapeDtypeStruct((NB, NT, D), F32),
            jax.ShapeDtypeStruct((B, POOL_HIST, E), F32),
            jax.ShapeDtypeStruct((POOL_HIST, NB, E), F32),
        ],
        scratch_shapes=[
            pltpu.VMEM((POOL_CARRY_ROWS, E), F32),
            pltpu.VMEM((tm, E), BF16),
            pltpu.VMEM((m, D), F32),
            pltpu.VMEM((m, D), BF16),
            pltpu.VMEM((1, 3, tm, CHUNK), F32),
            pltpu.VMEM((1, 3, tm, CHUNK), F32),
            pltpu.VMEM(win1.shape, BF16),
            pltpu.VMEM(wgrp.shape, BF16),
            pltpu.VMEM(wout1.shape, BF16),
        ] + weight_stage,
        compiler_params=pltpu.CompilerParams(
            dimension_semantics=("arbitrary",), vmem_limit_bytes=pool_limit),
        name="pool_layer",
    )(x1p, x1s, sp, g, gf, win1, wgrp, ps, wout1, stage_slot)
    return yp, ys, ncp, ncs, npp, nps


def kernel(x_prompt, x_sample, state_conv, state_pool, norm_g, final_norm_g,
           conv_w_in, conv_w, conv_b, conv_w_out,
           pool_w_in, pool_w_grp, pool_scale, pool_w_out):
    assert norm_g.shape[0] == 2 and conv_w_in.shape[0] == 1 and pool_w_in.shape[0] == 1
    gf = final_norm_g.reshape(1, D_MODEL)
    win0, wout0, win1, wout1 = conv_w_in[0], conv_w_out[0], pool_w_in[0], pool_w_out[0]
    wgrp = pool_w_grp[0].reshape(EXPAND_WIDTH, POOL_GROUP_WIDTH)
    yp, ys, ncp, ncs, npp, nps = _layer_calls(
        x_prompt, x_sample, state_conv[0], jnp.transpose(state_pool[0], (1, 0, 2)), norm_g, gf,
        win0, conv_w[0], conv_b, wout0, win1, wgrp, pool_scale, wout1)
    return (yp, ys, ncp[None], ncs[None], npp[None], jnp.transpose(nps, (1, 0, 2))[None])
```

```python
import functools

import jax
import jax.numpy as jnp
from jax import lax
from jax.experimental import pallas as pl
from jax.experimental.pallas import tpu as pltpu

D_MODEL = 1024
EXPAND_WIDTH = 2048
CONV_WIDTH = 3
POOL_WINDOWS = (2, 4, 8, 16)
N_POOL_GROUPS = len(POOL_WINDOWS)
POOL_GROUP_WIDTH = EXPAND_WIDTH // N_POOL_GROUPS
POOL_HIST = max(POOL_WINDOWS) - 1
PAST_LEN = 16384
RMS_EPS = 1e-6
LOG2_E = 1.4426950408889634

LANES = 128
SUBLANES = 8
CONV_CARRY_ROWS = SUBLANES
POOL_CARRY_ROWS = 2 * SUBLANES
CHUNK = POOL_GROUP_WIDTH
N_CHUNKS = EXPAND_WIDTH // CHUNK
PROMPT_TILE = 512
SAMPLE_SPLIT = 2
WEIGHT_PIECE = (1024, 512)
WEIGHT_STAGE_SLOTS = 3
VMEM_PHYSICAL_BYTES = 64 * 1024 * 1024

BF16 = jnp.bfloat16
F32 = jnp.float32


def _dot(a, b):
    return jnp.dot(a, b, preferred_element_type=F32)


def _rmsnorm(x, g):
    r = lax.rsqrt(jnp.mean(x * x, axis=-1, keepdims=True) + RMS_EPS)
    return x * r * g


def _silu(z):
    return z / (1.0 + jnp.exp2(z * (-LOG2_E)))


def _chunk(j, section=0):
    lo = section * EXPAND_WIDTH + j * CHUNK
    return slice(lo, lo + CHUNK)


def _weight_pieces(src_hbm, dst_ref):
    pr, pc = WEIGHT_PIECE
    rows, cols = src_hbm.shape
    assert rows % pr == 0 and cols % pc == 0
    pairs = []
    for i in range(rows // pr):
        for j in range(cols // pc):
            r, c = pl.ds(i * pr, pr), pl.ds(j * pc, pc)
            pairs.append((src_hbm.at[r, c], dst_ref.at[r, c]))
    return pairs


class _WeightStream:
    def __init__(self, pieces, stage_ref, sem_ref):
        self.pieces, self.stage, self.sem = pieces, stage_ref, sem_ref
        self.slots = stage_ref.shape[0]
        self.done = 0

    def _copy(self, k):
        return pltpu.make_async_copy(self.pieces[k][0], self.stage.at[k % self.slots],
                                     self.sem.at[k % self.slots])

    def start(self):
        for k in range(min(self.slots, len(self.pieces))):
            self._copy(k).start()

    def need(self, n):
        while self.done < n:
            k = self.done
            self._copy(k).wait()
            self.pieces[k][1][...] = self.stage[k % self.slots].astype(BF16)
            if k + self.slots < len(self.pieces):
                self._copy(k + self.slots).start()
            self.done += 1


def _time_major(ref, nt):
    return jnp.concatenate([ref[:, t, :] for t in range(nt)], axis=0)


def _conv_gates(hn, win_ref, j, stage):
    raw_refs, slot = stage
    raw_ref = raw_refs[j % len(raw_refs)]
    m = hn.shape[0]
    for sec in range(4):
        raw_ref[slot, sec, 0:m, :] = _dot(hn, win_ref[:, _chunk(j, sec)])
    gb, gc, v, z = (raw_ref[slot, sec, 0:m, :] for sec in range(4))
    return gb, gc * v, z


def _conv_prompt_tile(i, tm, x_ref, g_ref, win_ref, cw_ref, cb_ref, wout_ref,
                      x1_ref, nconv_ref, cvh_ref, ybuf_ref, stage, last_tile, weights=None):
    @pl.when(i == 0)
    def _():
        cvh_ref[...] = jnp.zeros_like(cvh_ref)

    x = x_ref[...]
    hn = _rmsnorm(x, g_ref[0:1, :]).astype(BF16)
    for j in range(N_CHUNKS):
        c = _chunk(j)
        if weights is not None:
            weights.need(weights.after_chunk[j])
        gb, cv, z = _conv_gates(hn, win_ref, j, stage)
        ext = jnp.concatenate([cvh_ref[:, c], cv], axis=0)
        cm1 = pltpu.roll(ext, 1, 0)[CONV_CARRY_ROWS:]
        cm2 = pltpu.roll(ext, 2, 0)[CONV_CARRY_ROWS:]
        conv = cb_ref[:, c] + cm2 * cw_ref[0:1, c]
        conv = conv + cm1 * cw_ref[1:2, c]
        conv = conv + cv * cw_ref[2:3, c]
        ybuf_ref[0:tm, c] = (gb * conv * _silu(z)).astype(BF16)
        cvh_ref[:, c] = cv[tm - CONV_CARRY_ROWS:]
    if weights is not None:
        weights.need(len(weights.pieces))
    x1_ref[...] = x + _dot(ybuf_ref[0:tm, :], wout_ref[...])

    @pl.when(last_tile)
    def _():
        nconv_ref[...] = cvh_ref[CONV_CARRY_ROWS - (CONV_WIDTH - 1):, :]


def _conv_sample_step(nb, nt, xs_ref, sc_ref, g_ref, win_ref, cw_ref, cb_ref, wout_ref,
                      x1s_ref, ncs_ref, ybuf_ref, stage):
    m = nt * nb
    xs = _time_major(xs_ref, nt)
    hn = _rmsnorm(xs, g_ref[0:1, :]).astype(BF16)
    for j in range(N_CHUNKS):
        c = _chunk(j)
        gb, cv, z = _conv_gates(hn, win_ref, j, stage)
        full = jnp.concatenate([sc_ref[:, k, c] for k in range(CONV_WIDTH - 1)] + [cv], axis=0)
        conv = cb_ref[:, c]
        for k in range(CONV_WIDTH):
            conv = conv + full[k * nb:(k + nt) * nb] * cw_ref[k:k + 1, c]
        ybuf_ref[0:m, c] = (gb * conv * _silu(z)).astype(BF16)
        for k in range(CONV_WIDTH - 1):
            ncs_ref[:, k, c] = full[(nt + k) * nb:(nt + k + 1) * nb]
    x1 = xs + _dot(ybuf_ref[0:m, :], wout_ref[...])
    for t in range(nt):
        x1s_ref[:, t, :] = x1[t * nb:(t + 1) * nb]


def _conv_layer_kernel(xp_ref, xs_ref, sc_ref, g_ref, win_hbm, cw_ref, cb_ref, wout_hbm, slot_ref,
                       x1p_ref, x1s_ref, ncp_ref, ncs_ref, cvh_ref, ybuf_ref, rawa_ref, rawb_ref,
                       win_ref, wout_ref, wstage_ref, wsem_ref,
                       *, tm, tiles_per_seq, n_prompt_steps, nb, nt):
    s = pl.program_id(0)
    stage = ((rawa_ref, rawb_ref), slot_ref[0])

    @pl.when(s == 0)
    def _():
        w_in = _weight_pieces(win_hbm, win_ref)
        order = [w_in[sec * N_CHUNKS + j] for j in range(N_CHUNKS) for sec in range(4)]
        weights = _WeightStream(order + _weight_pieces(wout_hbm, wout_ref), wstage_ref, wsem_ref)
        weights.after_chunk = [4 * (j + 1) for j in range(N_CHUNKS)]
        weights.start()
        _conv_prompt_tile(s, tm, xp_ref, g_ref, win_ref, cw_ref, cb_ref, wout_ref,
                          x1p_ref, ncp_ref, cvh_ref, ybuf_ref, stage, s == tiles_per_seq - 1, weights)

    @pl.when(jnp.logical_and(s > 0, s < n_prompt_steps))
    def _():
        i = s % tiles_per_seq
        _conv_prompt_tile(i, tm, xp_ref, g_ref, win_ref, cw_ref, cb_ref, wout_ref,
                          x1p_ref, ncp_ref, cvh_ref, ybuf_ref, stage, i == tiles_per_seq - 1)

    @pl.when(s >= n_prompt_steps)
    def _():
        _conv_sample_step(nb, nt, xs_ref, sc_ref, g_ref, win_ref, cw_ref, cb_ref, wout_ref,
                          x1s_ref, ncs_ref, ybuf_ref, stage)


def _pool_prompt_tile(i, tm, x1_ref, g_ref, gf_ref, win_ref, wgrp_ref, ps_ref, wout_ref,
                      y_ref, npool_ref, uh_ref, ybuf_ref, stage, last_tile, weights=None):
    @pl.when(i == 0)
    def _():
        uh_ref[...] = jnp.zeros_like(uh_ref)

    x1 = x1_ref[...]
    hn1 = _rmsnorm(x1, g_ref[1:2, :]).astype(BF16)
    seen = (i * tm + 1 + lax.broadcasted_iota(jnp.int32, (tm, LANES), 0)).astype(F32)
    raw_refs, slot = stage
    for g, w in enumerate(POOL_WINDOWS):
        c = _chunk(g)
        raw_ref = raw_refs[g % len(raw_refs)]
        if weights is not None:
            weights.need(weights.after_group[g])
        raw_ref[slot, 0] = _dot(hn1, win_ref[:, _chunk(g, 0)])
        raw_ref[slot, 1] = _dot(hn1, win_ref[:, _chunk(g, 1)])
        u = raw_ref[slot, 0]
        s = jnp.concatenate([uh_ref[:, c], u], axis=0)
        shift = 1
        while shift < w:
            s = s + pltpu.roll(s, shift, 0)
            shift *= 2
        inv = 1.0 / jnp.minimum(jnp.float32(w), seen)
        inv = jnp.concatenate([inv] * (CHUNK // LANES), axis=1)
        p = s[POOL_CARRY_ROWS:] * inv - u
        raw_ref[slot, 2] = _dot(p.astype(BF16), wgrp_ref[_chunk(g), :])
        q, z = raw_ref[slot, 2], raw_ref[slot, 1]
        ybuf_ref[0:tm, c] = (q * ps_ref[:, c] * _silu(z)).astype(BF16)
        uh_ref[:, c] = u[tm - POOL_CARRY_ROWS:]
    kh, mh = EXPAND_WIDTH // 2, tm // 2
    rows = (slice(0, mh), slice(mh, tm))
    if weights is not None:
        weights.need(len(weights.pieces))
    lo = [_dot(ybuf_ref[r, 0:kh], wout_ref[0:kh, :]) for r in rows]
    hi = [_dot(ybuf_ref[r, kh:], wout_ref[kh:, :]) for r in rows]
    for r, a, b in zip(rows, lo, hi):
        y_ref[r, :] = _rmsnorm(x1_ref[r, :] + (a + b), gf_ref[...])

    @pl.when(last_tile)
    def _():
        npool_ref[...] = uh_ref[POOL_CARRY_ROWS - POOL_HIST:, :]


def _pool_sample_step(g, nb, nt, start_pos, x1s_ref, sp_ref, g_ref, gf_ref, win_ref, wgrp_ref, ps_ref,
                      wout_ref, ys_ref, nps_ref, acc_ref, hn_ref, stage):
    w = POOL_WINDOWS[g]
    c = _chunk(g)
    m = nt * nb
    if g == 0:
        x1 = _time_major(x1s_ref, nt)
        acc_ref[...] = x1
        hn_ref[...] = _rmsnorm(x1, g_ref[1:2, :]).astype(BF16)
    hn1 = hn_ref[...]
    raw_refs, slot = stage
    raw_ref = raw_refs[g % len(raw_refs)]
    raw_ref[slot, 0, 0:m, :] = _dot(hn1, win_ref[:, _chunk(g, 0)])
    raw_ref[slot, 1, 0:m, :] = _dot(hn1, win_ref[:, _chunk(g, 1)])
    u = raw_ref[slot, 0, 0:m, :]

    def full(r):
        if r < POOL_HIST:
            return sp_ref[r]
        return u[(r - POOL_HIST) * nb:(r - POOL_HIST + 1) * nb]

    parts = []
    for t in range(nt):
        win = full(POOL_HIST + t)
        for k in range(1, w):
            win = win + full(POOL_HIST + t - k)
        inv = 1.0 / float(min(w, start_pos + t + 1))
        parts.append(win * inv - full(POOL_HIST + t))
    p = jnp.concatenate(parts, axis=0).astype(BF16)
    raw_ref[slot, 2, 0:m, :] = _dot(p, wgrp_ref[_chunk(g), :])
    q, z = raw_ref[slot, 2, 0:m, :], raw_ref[slot, 1, 0:m, :]
    y = (q * ps_ref[:, c] * _silu(z)).astype(BF16)
    acc_ref[...] += _dot(y, wout_ref[c, :])

    for r in range(POOL_HIST):
        nps_ref[r] = full(r + nt)
    if g == N_POOL_GROUPS - 1:
        ys = _rmsnorm(acc_ref[...], gf_ref[...])
        for t in range(nt):
            ys_ref[:, t, :] = ys[t * nb:(t + 1) * nb]


def _pool_layer_kernel(x1p_ref, x1s_ref, sp_ref, g_ref, gf_ref, win_hbm, wgrp_hbm, ps_ref, wout_hbm, slot_ref,
                       yp_ref, ys_ref, npp_ref, nps_ref, uh_ref, ybuf_ref, acc_ref, hn_ref,
                       rawa_ref, rawb_ref, win_ref, wgrp_ref, wout_ref, wstage_ref, wsem_ref,
                       *, tm, tiles_per_seq, n_prompt_steps, nb, nt, start_pos):
    s = pl.program_id(0)
    stage = ((rawa_ref, rawb_ref), slot_ref[0])
    phase = jnp.maximum(s - n_prompt_steps, 0) % N_POOL_GROUPS

    @pl.when(s == 0)
    def _():
        w_in, w_grp = _weight_pieces(win_hbm, win_ref), _weight_pieces(wgrp_hbm, wgrp_ref)
        groups_per_piece = WEIGHT_PIECE[0] // CHUNK
        order, after_group = [], []
        for g in range(N_POOL_GROUPS):
            order += [w_in[g], w_in[N_POOL_GROUPS + g]]
            if g % groups_per_piece == 0:
                order.append(w_grp[g // groups_per_piece])
            after_group.append(len(order))
        weights = _WeightStream(order + _weight_pieces(wout_hbm, wout_ref), wstage_ref, wsem_ref)
        weights.after_group = after_group
        weights.start()
        _pool_prompt_tile(s, tm, x1p_ref, g_ref, gf_ref, win_ref, wgrp_ref, ps_ref, wout_ref,
                          yp_ref, npp_ref, uh_ref, ybuf_ref, stage, s == tiles_per_seq - 1, weights)

    @pl.when(jnp.logical_and(s > 0, s < n_prompt_steps))
    def _():
        i = s % tiles_per_seq
        _pool_prompt_tile(i, tm, x1p_ref, g_ref, gf_ref, win_ref, wgrp_ref, ps_ref, wout_ref,
                          yp_ref, npp_ref, uh_ref, ybuf_ref, stage, i == tiles_per_seq - 1)

    for g in range(N_POOL_GROUPS):
        @pl.when(jnp.logical_and(s >= n_prompt_steps, phase == g))
        def _(g=g):
            _pool_sample_step(g, nb, nt, start_pos, x1s_ref, sp_ref, g_ref, gf_ref, win_ref, wgrp_ref,
                              ps_ref, wout_ref, ys_ref, nps_ref, acc_ref, hn_ref, stage)


def _resident(shape):
    zeros = (0,) * len(shape)
    return pl.BlockSpec(shape, lambda s: zeros, pipeline_mode=pl.Buffered(1))


def _layer_calls(xp, xs, sc, sp, g, gf, win0, cw, cb, wout0, win1, wgrp, ps, wout1):
    B, S, D = xp.shape
    NB, NT, _ = xs.shape
    E = EXPAND_WIDTH
    tm = PROMPT_TILE
    assert S % tm == 0 and tm >= POOL_CARRY_ROWS and NB % SAMPLE_SPLIT == 0
    tiles = S // tm
    n_prompt = B * tiles
    nb = NB // SAMPLE_SPLIT
    m = NT * nb
    assert nb % SUBLANES == 0 and m <= tm

    def prompt_tile(s):
        sc_ = jnp.minimum(s, n_prompt - 1)
        return (sc_ // tiles, sc_ % tiles, 0)

    def prompt_seq(s):
        return (jnp.minimum(s, n_prompt - 1) // tiles, 0, 0)

    def conv_part(s):
        return (jnp.maximum(s - n_prompt, 0), 0, 0)

    def pool_part(s):
        return (jnp.maximum(s - n_prompt, 0) // N_POOL_GROUPS, 0, 0)

    def pool_group(s):
        q = jnp.maximum(s - n_prompt, 0)
        return (0, q // N_POOL_GROUPS, q % N_POOL_GROUPS)

    tile_bytes = 4 * tm * D * 4
    temp_bytes = 16 * tm * CHUNK * 4 + tm * E * 2
    stage_bytes = 2 * 4 * tm * CHUNK * 4
    headroom = 4 << 20
    stage_slot = jnp.zeros((1,), jnp.int32)
    hbm = pl.BlockSpec(memory_space=pl.ANY)
    weight_stage = [pltpu.VMEM((WEIGHT_STAGE_SLOTS,) + WEIGHT_PIECE, F32),
                    pltpu.SemaphoreType.DMA((WEIGHT_STAGE_SLOTS,))]
    weight_stage_bytes = WEIGHT_STAGE_SLOTS * WEIGHT_PIECE[0] * WEIGHT_PIECE[1] * 4

    conv_sample_bytes = 2 * 4 * (2 * NT * nb * D + 2 * (CONV_WIDTH - 1) * nb * E)
    conv_limit = min(2 * (win0.size + wout0.size) + weight_stage_bytes + tile_bytes + temp_bytes + stage_bytes
                     + conv_sample_bytes + headroom, VMEM_PHYSICAL_BYTES - headroom)
    x1p, x1s, ncp, ncs = pl.pallas_call(
        functools.partial(_conv_layer_kernel, tm=tm, tiles_per_seq=tiles, n_prompt_steps=n_prompt,
                          nb=nb, nt=NT),
        grid=(n_prompt + SAMPLE_SPLIT,),
        in_specs=[
            pl.BlockSpec((None, tm, D), prompt_tile),
            pl.BlockSpec((nb, NT, D), conv_part),
            pl.BlockSpec((nb, CONV_WIDTH - 1, E), conv_part),
            _resident(g.shape), hbm, _resident(cw.shape), _resident(cb.shape), hbm,
            pl.BlockSpec(memory_space=pltpu.SMEM),
        ],
        out_specs=[
            pl.BlockSpec((None, tm, D), prompt_tile),
            pl.BlockSpec((nb, NT, D), conv_part),
            pl.BlockSpec((None, CONV_WIDTH - 1, E), prompt_seq),
            pl.BlockSpec((nb, CONV_WIDTH - 1, E), conv_part),
        ],
        out_shape=[
            jax.ShapeDtypeStruct((B, S, D), F32),
            jax.ShapeDtypeStruct((NB, NT, D), F32),
            jax.ShapeDtypeStruct((B, CONV_WIDTH - 1, E), F32),
            jax.ShapeDtypeStruct((NB, CONV_WIDTH - 1, E), F32),
        ],
        scratch_shapes=[
            pltpu.VMEM((CONV_CARRY_ROWS, E), F32),
            pltpu.VMEM((tm, E), BF16),
            pltpu.VMEM((1, 4, tm, CHUNK), F32),
            pltpu.VMEM((1, 4, tm, CHUNK), F32),
            pltpu.VMEM(win0.shape, BF16),
            pltpu.VMEM(wout0.shape, BF16),
        ] + weight_stage,
        compiler_params=pltpu.CompilerParams(
            dimension_semantics=("arbitrary",), vmem_limit_bytes=conv_limit),
        name="conv_layer",
    )(xp, xs, sc, g, win0, cw, cb, wout0, stage_slot)

    pool_sample_bytes = 2 * 4 * (2 * NT * nb * D + 2 * nb * (POOL_HIST + 1) * CHUNK) + m * D * 6
    pool_limit = min(2 * (win1.size + wgrp.size + wout1.size) + weight_stage_bytes + tile_bytes + temp_bytes
                     + stage_bytes + pool_sample_bytes + headroom, VMEM_PHYSICAL_BYTES - headroom)
    yp, ys, npp, nps = pl.pallas_call(
        functools.partial(_pool_layer_kernel, tm=tm, tiles_per_seq=tiles, n_prompt_steps=n_prompt,
                          nb=nb, nt=NT, start_pos=PAST_LEN),
        grid=(n_prompt + SAMPLE_SPLIT * N_POOL_GROUPS,),
        in_specs=[
            pl.BlockSpec((None, tm, D), prompt_tile),
            pl.BlockSpec((nb, NT, D), pool_part),
            pl.BlockSpec((POOL_HIST, nb, CHUNK), pool_group),
            _resident(g.shape), _resident(gf.shape), hbm, hbm, _resident(ps.shape), hbm,
            pl.BlockSpec(memory_space=pltpu.SMEM),
        ],
        out_specs=[
            pl.BlockSpec((None, tm, D), prompt_tile),
            pl.BlockSpec((nb, NT, D), pool_part),
            pl.BlockSpec((None, POOL_HIST, E), prompt_seq),
            pl.BlockSpec((POOL_HIST, nb, CHUNK), pool_group),
        ],
        out_shape=[
            jax.ShapeDtypeStruct((B, S, D), F32),
            jax.ShapeDtypeStruct((NB, NT, D), F32),
            jax.ShapeDtypeStruct((B, POOL_HIST, E), F32),
            jax.ShapeDtypeStruct((POOL_HIST, NB, E), F32),
        ],
        scratch_shapes=[
            pltpu.VMEM((POOL_CARRY_ROWS, E), F32),
            pltpu.VMEM((tm, E), BF16),
            pltpu.VMEM((m, D), F32),
            pltpu.VMEM((m, D), BF16),
            pltpu.VMEM((1, 3, tm, CHUNK), F32),
            pltpu.VMEM((1, 3, tm, CHUNK), F32),
            pltpu.VMEM(win1.shape, BF16),
            pltpu.VMEM(wgrp.shape, BF16),
            pltpu.VMEM(wout1.shape, BF16),
        ] + weight_stage,
        compiler_params=pltpu.CompilerParams(
            dimension_semantics=("arbitrary",), vmem_limit_bytes=pool_limit),
        name="pool_layer",
    )(x1p, x1s, sp, g, gf, win1, wgrp, ps, wout1, stage_slot)
    return yp, ys, ncp, ncs, npp, nps


def kernel(x_prompt, x_sample, state_conv, state_pool, norm_g, final_norm_g,
           conv_w_in, conv_w, conv_b, conv_w_out,
           pool_w_in, pool_w_grp, pool_scale, pool_w_out):
    assert norm_g.shape[0] == 2 and conv_w_in.shape[0] == 1 and pool_w_in.shape[0] == 1
    gf = final_norm_g.reshape(1, D_MODEL)
    win0, wout0, win1, wout1 = conv_w_in[0], conv_w_out[0], pool_w_in[0], pool_w_out[0]
    wgrp = pool_w_grp[0].reshape(EXPAND_WIDTH, POOL_GROUP_WIDTH)
    yp, ys, ncp, ncs, npp, nps = _layer_calls(
        x_prompt, x_sample, state_conv[0], jnp.transpose(state_pool[0], (1, 0, 2)), norm_g, gf,
        win0, conv_w[0], conv_b, wout0, win1, wgrp, pool_scale, wout1)
    return (yp, ys, ncp[None], ncs[None], npp[None], jnp.transpose(nps, (1, 0, 2))[None])
```

```python
import functools

import jax
import jax.numpy as jnp
from jax import lax
from jax.experimental import pallas as pl
from jax.experimental.pallas import tpu as pltpu

D_MODEL = 1024
EXPAND_WIDTH = 2048
CONV_WIDTH = 3
POOL_WINDOWS = (2, 4, 8, 16)
N_POOL_GROUPS = len(POOL_WINDOWS)
POOL_GROUP_WIDTH = EXPAND_WIDTH // N_POOL_GROUPS
POOL_HIST = max(POOL_WINDOWS) - 1
PAST_LEN = 16384
RMS_EPS = 1e-6
LOG2_E = 1.4426950408889634

LANES = 128
SUBLANES = 8
CONV_CARRY_ROWS = SUBLANES
POOL_CARRY_ROWS = 2 * SUBLANES
CHUNK = POOL_GROUP_WIDTH
N_CHUNKS = EXPAND_WIDTH // CHUNK
PROMPT_TILE = 512
SAMPLE_SPLIT = 2
VMEM_PHYSICAL_BYTES = 64 * 1024 * 1024

BF16 = jnp.bfloat16
F32 = jnp.float32


def _dot(a, b):
    return jnp.dot(a, b, preferred_element_type=F32)


def _rmsnorm(x, g):
    r = lax.rsqrt(jnp.mean(x * x, axis=-1, keepdims=True) + RMS_EPS)
    return x * r * g


def _silu(z):
    return z / (1.0 + jnp.exp2(z * (-LOG2_E)))


def _chunk(j, section=0):
    lo = section * EXPAND_WIDTH + j * CHUNK
    return slice(lo, lo + CHUNK)


def _weight_pieces(src_hbm, dst_ref, piece):
    pr, pc = piece
    rows, cols = src_hbm.shape
    assert rows % pr == 0 and cols % pc == 0
    pairs = []
    for i in range(rows // pr):
        for j in range(cols // pc):
            r, c = pl.ds(i * pr, pr), pl.ds(j * pc, pc)
            pairs.append((src_hbm.at[r, c], dst_ref.at[r, c]))
    return pairs


def _load_weights_as_bf16(weights, raw_refs, sem_ref):
    slots = [r.at[0, k] for r in raw_refs for k in range(r.shape[1])]
    pieces = []
    for src_hbm, dst_ref in weights:
        pieces += _weight_pieces(src_hbm, dst_ref, slots[0].shape)
    n, r = len(pieces), len(slots)

    def copy(k):
        return pltpu.make_async_copy(pieces[k][0], slots[k % r], sem_ref.at[k % r])

    for k in range(min(r, n)):
        copy(k).start()
    for k in range(n):
        copy(k).wait()
        pieces[k][1][...] = slots[k % r][...].astype(BF16)
        if k + r < n:
            copy(k + r).start()


def _time_major(ref, nt):
    return jnp.concatenate([ref[:, t, :] for t in range(nt)], axis=0)


def _conv_gates(hn, win_ref, j, stage):
    raw_refs, slot = stage
    raw_ref = raw_refs[j % len(raw_refs)]
    m = hn.shape[0]
    for sec in range(4):
        raw_ref[slot, sec, 0:m, :] = _dot(hn, win_ref[:, _chunk(j, sec)])
    gb, gc, v, z = (raw_ref[slot, sec, 0:m, :] for sec in range(4))
    return gb, gc * v, z


def _conv_prompt_tile(i, tm, x_ref, g_ref, win_ref, cw_ref, cb_ref, wout_ref,
                      x1_ref, nconv_ref, cvh_ref, ybuf_ref, stage, last_tile):
    @pl.when(i == 0)
    def _():
        cvh_ref[...] = jnp.zeros_like(cvh_ref)

    x = x_ref[...]
    hn = _rmsnorm(x, g_ref[0:1, :]).astype(BF16)
    for j in range(N_CHUNKS):
        c = _chunk(j)
        gb, cv, z = _conv_gates(hn, win_ref, j, stage)
        ext = jnp.concatenate([cvh_ref[:, c], cv], axis=0)
        cm1 = pltpu.roll(ext, 1, 0)[CONV_CARRY_ROWS:]
        cm2 = pltpu.roll(ext, 2, 0)[CONV_CARRY_ROWS:]
        conv = cb_ref[:, c] + cm2 * cw_ref[0:1, c]
        conv = conv + cm1 * cw_ref[1:2, c]
        conv = conv + cv * cw_ref[2:3, c]
        ybuf_ref[0:tm, c] = (gb * conv * _silu(z)).astype(BF16)
        cvh_ref[:, c] = cv[tm - CONV_CARRY_ROWS:]
    x1_ref[...] = x + _dot(ybuf_ref[0:tm, :], wout_ref[...])

    @pl.when(last_tile)
    def _():
        nconv_ref[...] = cvh_ref[CONV_CARRY_ROWS - (CONV_WIDTH - 1):, :]


def _conv_sample_step(nb, nt, xs_ref, sc_ref, g_ref, win_ref, cw_ref, cb_ref, wout_ref,
                      x1s_ref, ncs_ref, ybuf_ref, stage):
    m = nt * nb
    xs = _time_major(xs_ref, nt)
    hn = _rmsnorm(xs, g_ref[0:1, :]).astype(BF16)
    for j in range(N_CHUNKS):
        c = _chunk(j)
        gb, cv, z = _conv_gates(hn, win_ref, j, stage)
        full = jnp.concatenate([sc_ref[:, k, c] for k in range(CONV_WIDTH - 1)] + [cv], axis=0)
        conv = cb_ref[:, c]
        for k in range(CONV_WIDTH):
            conv = conv + full[k * nb:(k + nt) * nb] * cw_ref[k:k + 1, c]
        ybuf_ref[0:m, c] = (gb * conv * _silu(z)).astype(BF16)
        for k in range(CONV_WIDTH - 1):
            ncs_ref[:, k, c] = full[(nt + k) * nb:(nt + k + 1) * nb]
    x1 = xs + _dot(ybuf_ref[0:m, :], wout_ref[...])
    for t in range(nt):
        x1s_ref[:, t, :] = x1[t * nb:(t + 1) * nb]


def _conv_layer_kernel(xp_ref, xs_ref, sc_ref, g_ref, win_hbm, cw_ref, cb_ref, wout_hbm, slot_ref,
                       x1p_ref, x1s_ref, ncp_ref, ncs_ref, cvh_ref, ybuf_ref, rawa_ref, rawb_ref,
                       win_ref, wout_ref, wsem_ref,
                       *, tm, tiles_per_seq, n_prompt_steps, nb, nt):
    s = pl.program_id(0)
    raw_refs = (rawa_ref, rawb_ref)
    stage = (raw_refs, slot_ref[0])

    @pl.when(s == 0)
    def _():
        _load_weights_as_bf16([(win_hbm, win_ref), (wout_hbm, wout_ref)], raw_refs, wsem_ref)

    @pl.when(s < n_prompt_steps)
    def _():
        i = s % tiles_per_seq
        _conv_prompt_tile(i, tm, xp_ref, g_ref, win_ref, cw_ref, cb_ref, wout_ref,
                          x1p_ref, ncp_ref, cvh_ref, ybuf_ref, stage, i == tiles_per_seq - 1)

    @pl.when(s >= n_prompt_steps)
    def _():
        _conv_sample_step(nb, nt, xs_ref, sc_ref, g_ref, win_ref, cw_ref, cb_ref, wout_ref,
                          x1s_ref, ncs_ref, ybuf_ref, stage)


def _pool_prompt_tile(i, tm, x1_ref, g_ref, gf_ref, win_ref, wgrp_ref, ps_ref, wout_ref,
                      y_ref, npool_ref, uh_ref, ybuf_ref, stage, last_tile):
    @pl.when(i == 0)
    def _():
        uh_ref[...] = jnp.zeros_like(uh_ref)

    x1 = x1_ref[...]
    hn1 = _rmsnorm(x1, g_ref[1:2, :]).astype(BF16)
    seen = (i * tm + 1 + lax.broadcasted_iota(jnp.int32, (tm, LANES), 0)).astype(F32)
    raw_refs, slot = stage
    for g, w in enumerate(POOL_WINDOWS):
        c = _chunk(g)
        raw_ref = raw_refs[g % len(raw_refs)]
        raw_ref[slot, 0] = _dot(hn1, win_ref[:, _chunk(g, 0)])
        raw_ref[slot, 1] = _dot(hn1, win_ref[:, _chunk(g, 1)])
        u = raw_ref[slot, 0]
        s = jnp.concatenate([uh_ref[:, c], u], axis=0)
        shift = 1
        while shift < w:
            s = s + pltpu.roll(s, shift, 0)
            shift *= 2
        inv = 1.0 / jnp.minimum(jnp.float32(w), seen)
        inv = jnp.concatenate([inv] * (CHUNK // LANES), axis=1)
        p = s[POOL_CARRY_ROWS:] * inv - u
        raw_ref[slot, 2] = _dot(p.astype(BF16), wgrp_ref[_chunk(g), :])
        q, z = raw_ref[slot, 2], raw_ref[slot, 1]
        ybuf_ref[0:tm, c] = (q * ps_ref[:, c] * _silu(z)).astype(BF16)
        uh_ref[:, c] = u[tm - POOL_CARRY_ROWS:]
    kh, mh = EXPAND_WIDTH // 2, tm // 2
    rows = (slice(0, mh), slice(mh, tm))
    lo = [_dot(ybuf_ref[r, 0:kh], wout_ref[0:kh, :]) for r in rows]
    hi = [_dot(ybuf_ref[r, kh:], wout_ref[kh:, :]) for r in rows]
    for r, a, b in zip(rows, lo, hi):
        y_ref[r, :] = _rmsnorm(x1_ref[r, :] + (a + b), gf_ref[...])

    @pl.when(last_tile)
    def _():
        npool_ref[...] = uh_ref[POOL_CARRY_ROWS - POOL_HIST:, :]


def _pool_sample_step(g, nb, nt, start_pos, x1s_ref, sp_ref, g_ref, gf_ref, win_ref, wgrp_ref, ps_ref,
                      wout_ref, ys_ref, nps_ref, acc_ref, hn_ref, stage):
    w = POOL_WINDOWS[g]
    c = _chunk(g)
    m = nt * nb
    if g == 0:
        x1 = _time_major(x1s_ref, nt)
        acc_ref[...] = x1
        hn_ref[...] = _rmsnorm(x1, g_ref[1:2, :]).astype(BF16)
    hn1 = hn_ref[...]
    raw_refs, slot = stage
    raw_ref = raw_refs[g % len(raw_refs)]
    raw_ref[slot, 0, 0:m, :] = _dot(hn1, win_ref[:, _chunk(g, 0)])
    raw_ref[slot, 1, 0:m, :] = _dot(hn1, win_ref[:, _chunk(g, 1)])
    u = raw_ref[slot, 0, 0:m, :]

    def full(r):
        if r < POOL_HIST:
            return sp_ref[r]
        return u[(r - POOL_HIST) * nb:(r - POOL_HIST + 1) * nb]

    parts = []
    for t in range(nt):
        win = full(POOL_HIST + t)
        for k in range(1, w):
            win = win + full(POOL_HIST + t - k)
        inv = 1.0 / float(min(w, start_pos + t + 1))
        parts.append(win * inv - full(POOL_HIST + t))
    p = jnp.concatenate(parts, axis=0).astype(BF16)
    raw_ref[slot, 2, 0:m, :] = _dot(p, wgrp_ref[_chunk(g), :])
    q, z = raw_ref[slot, 2, 0:m, :], raw_ref[slot, 1, 0:m, :]
    y = (q * ps_ref[:, c] * _silu(z)).astype(BF16)
    acc_ref[...] += _dot(y, wout_ref[c, :])

    for r in range(POOL_HIST):
        nps_ref[r] = full(r + nt)
    if g == N_POOL_GROUPS - 1:
        ys = _rmsnorm(acc_ref[...], gf_ref[...])
        for t in range(nt):
            ys_ref[:, t, :] = ys[t * nb:(t + 1) * nb]


def _pool_layer_kernel(x1p_ref, x1s_ref, sp_ref, g_ref, gf_ref, win_hbm, wgrp_hbm, ps_ref, wout_hbm, slot_ref,
                       yp_ref, ys_ref, npp_ref, nps_ref, uh_ref, ybuf_ref, acc_ref, hn_ref,
                       rawa_ref, rawb_ref, win_ref, wgrp_ref, wout_ref, wsem_ref,
                       *, tm, tiles_per_seq, n_prompt_steps, nb, nt, start_pos):
    s = pl.program_id(0)
    raw_refs = (rawa_ref, rawb_ref)
    stage = (raw_refs, slot_ref[0])
    phase = jnp.maximum(s - n_prompt_steps, 0) % N_POOL_GROUPS

    @pl.when(s == 0)
    def _():
        _load_weights_as_bf16([(win_hbm, win_ref), (wgrp_hbm, wgrp_ref), (wout_hbm, wout_ref)],
                              raw_refs, wsem_ref)

    @pl.when(s < n_prompt_steps)
    def _():
        i = s % tiles_per_seq
        _pool_prompt_tile(i, tm, x1p_ref, g_ref, gf_ref, win_ref, wgrp_ref, ps_ref, wout_ref,
                          yp_ref, npp_ref, uh_ref, ybuf_ref, stage, i == tiles_per_seq - 1)

    for g in range(N_POOL_GROUPS):
        @pl.when(jnp.logical_and(s >= n_prompt_steps, phase == g))
        def _(g=g):
            _pool_sample_step(g, nb, nt, start_pos, x1s_ref, sp_ref, g_ref, gf_ref, win_ref, wgrp_ref,
                              ps_ref, wout_ref, ys_ref, nps_ref, acc_ref, hn_ref, stage)


def _resident(shape):
    zeros = (0,) * len(shape)
    return pl.BlockSpec(shape, lambda s: zeros, pipeline_mode=pl.Buffered(1))


def _layer_calls(xp, xs, sc, sp, g, gf, win0, cw, cb, wout0, win1, wgrp, ps, wout1):
    B, S, D = xp.shape
    NB, NT, _ = xs.shape
    E = EXPAND_WIDTH
    tm = PROMPT_TILE
    assert S % tm == 0 and tm >= POOL_CARRY_ROWS and NB % SAMPLE_SPLIT == 0
    tiles = S // tm
    n_prompt = B * tiles
    nb = NB // SAMPLE_SPLIT
    m = NT * nb
    assert nb % SUBLANES == 0 and m <= tm

    def prompt_tile(s):
        sc_ = jnp.minimum(s, n_prompt - 1)
        return (sc_ // tiles, sc_ % tiles, 0)

    def prompt_seq(s):
        return (jnp.minimum(s, n_prompt - 1) // tiles, 0, 0)

    def conv_part(s):
        return (jnp.maximum(s - n_prompt, 0), 0, 0)

    def pool_part(s):
        return (jnp.maximum(s - n_prompt, 0) // N_POOL_GROUPS, 0, 0)

    def pool_group(s):
        q = jnp.maximum(s - n_prompt, 0)
        return (0, q // N_POOL_GROUPS, q % N_POOL_GROUPS)

    tile_bytes = 4 * tm * D * 4
    temp_bytes = 16 * tm * CHUNK * 4 + tm * E * 2
    stage_bytes = 2 * 4 * tm * CHUNK * 4
    headroom = 4 << 20
    stage_slot = jnp.zeros((1,), jnp.int32)
    hbm = pl.BlockSpec(memory_space=pl.ANY)
    conv_planes, pool_planes = 4, 3

    conv_sample_bytes = 2 * 4 * (2 * NT * nb * D + 2 * (CONV_WIDTH - 1) * nb * E)
    conv_limit = min(2 * (win0.size + wout0.size) + tile_bytes + temp_bytes + stage_bytes
                     + conv_sample_bytes + headroom, VMEM_PHYSICAL_BYTES - headroom)
    x1p, x1s, ncp, ncs = pl.pallas_call(
        functools.partial(_conv_layer_kernel, tm=tm, tiles_per_seq=tiles, n_prompt_steps=n_prompt,
                          nb=nb, nt=NT),
        grid=(n_prompt + SAMPLE_SPLIT,),
        in_specs=[
            pl.BlockSpec((None, tm, D), prompt_tile),
            pl.BlockSpec((nb, NT, D), conv_part),
            pl.BlockSpec((nb, CONV_WIDTH - 1, E), conv_part),
            _resident(g.shape), hbm, _resident(cw.shape), _resident(cb.shape), hbm,
            pl.BlockSpec(memory_space=pltpu.SMEM),
        ],
        out_specs=[
            pl.BlockSpec((None, tm, D), prompt_tile),
            pl.BlockSpec((nb, NT, D), conv_part),
            pl.BlockSpec((None, CONV_WIDTH - 1, E), prompt_seq),
            pl.BlockSpec((nb, CONV_WIDTH - 1, E), conv_part),
        ],
        out_shape=[
            jax.ShapeDtypeStruct((B, S, D), F32),
            jax.ShapeDtypeStruct((NB, NT, D), F32),
            jax.ShapeDtypeStruct((B, CONV_WIDTH - 1, E), F32),
            jax.ShapeDtypeStruct((NB, CONV_WIDTH - 1, E), F32),
        ],
        scratch_shapes=[
            pltpu.VMEM((CONV_CARRY_ROWS, E), F32),
            pltpu.VMEM((tm, E), BF16),
            pltpu.VMEM((1, conv_planes, tm, CHUNK), F32),
            pltpu.VMEM((1, conv_planes, tm, CHUNK), F32),
            pltpu.VMEM(win0.shape, BF16),
            pltpu.VMEM(wout0.shape, BF16),
            pltpu.SemaphoreType.DMA((2 * conv_planes,)),
        ],
        compiler_params=pltpu.CompilerParams(
            dimension_semantics=("arbitrary",), vmem_limit_bytes=conv_limit),
        name="conv_layer",
    )(xp, xs, sc, g, win0, cw, cb, wout0, stage_slot)

    pool_sample_bytes = 2 * 4 * (2 * NT * nb * D + 2 * nb * (POOL_HIST + 1) * CHUNK) + m * D * 6
    pool_limit = min(2 * (win1.size + wgrp.size + wout1.size) + tile_bytes + temp_bytes
                     + stage_bytes + pool_sample_bytes + headroom, VMEM_PHYSICAL_BYTES - headroom)
    yp, ys, npp, nps = pl.pallas_call(
        functools.partial(_pool_layer_kernel, tm=tm, tiles_per_seq=tiles, n_prompt_steps=n_prompt,
                          nb=nb, nt=NT, start_pos=PAST_LEN),
        grid=(n_prompt + SAMPLE_SPLIT * N_POOL_GROUPS,),
        in_specs=[
            pl.BlockSpec((None, tm, D), prompt_tile),
            pl.BlockSpec((nb, NT, D), pool_part),
            pl.BlockSpec((POOL_HIST, nb, CHUNK), pool_group),
            _resident(g.shape), _resident(gf.shape), hbm, hbm, _resident(ps.shape), hbm,
            pl.BlockSpec(memory_space=pltpu.SMEM),
        ],
        out_specs=[
            pl.BlockSpec((None, tm, D), prompt_tile),
            pl.BlockSpec((nb, NT, D), pool_part),
            pl.BlockSpec((None, POOL_HIST, E), prompt_seq),
            pl.BlockSpec((POOL_HIST, nb, CHUNK), pool_group),
        ],
        out_shape=[
            jax.ShapeDtypeStruct((B, S, D), F32),
            jax.ShapeDtypeStruct((NB, NT, D), F32),
            jax.ShapeDtypeStruct((B, POOL_HIST, E), F32),
            jax.ShapeDtypeStruct((POOL_HIST, NB, E), F32),
        ],
        scratch_shapes=[
            pltpu.VMEM((POOL_CARRY_ROWS, E), F32),
            pltpu.VMEM((tm, E), BF16),
            pltpu.VMEM((m, D), F32),
            pltpu.VMEM((m, D), BF16),
            pltpu.VMEM((1, pool_planes, tm, CHUNK), F32),
            pltpu.VMEM((1, pool_planes, tm, CHUNK), F32),
            pltpu.VMEM(win1.shape, BF16),
            pltpu.VMEM(wgrp.shape, BF16),
            pltpu.VMEM(wout1.shape, BF16),
            pltpu.SemaphoreType.DMA((2 * pool_planes,)),
        ],
        compiler_params=pltpu.CompilerParams(
            dimension_semantics=("arbitrary",), vmem_limit_bytes=pool_limit),
        name="pool_layer",
    )(x1p, x1s, sp, g, gf, win1, wgrp, ps, wout1, stage_slot)
    return yp, ys, ncp, ncs, npp, nps


def kernel(x_prompt, x_sample, state_conv, state_pool, norm_g, final_norm_g,
           conv_w_in, conv_w, conv_b, conv_w_out,
           pool_w_in, pool_w_grp, pool_scale, pool_w_out):
    assert norm_g.shape[0] == 2 and conv_w_in.shape[0] == 1 and pool_w_in.shape[0] == 1
    gf = final_norm_g.reshape(1, D_MODEL)
    win0, wout0, win1, wout1 = conv_w_in[0], conv_w_out[0], pool_w_in[0], pool_w_out[0]
    wgrp = pool_w_grp[0].reshape(EXPAND_WIDTH, POOL_GROUP_WIDTH)
    yp, ys, ncp, ncs, npp, nps = _layer_calls(
        x_prompt, x_sample, state_conv[0], jnp.transpose(state_pool[0], (1, 0, 2)), norm_g, gf,
        win0, conv_w[0], conv_b, wout0, win1, wgrp, pool_scale, wout1)
    return (yp, ys, ncp[None], ncs[None], npp[None], jnp.transpose(nps, (1, 0, 2))[None])
```

```python
import functools

import jax
import jax.numpy as jnp
from jax import lax
from jax.experimental import pallas as pl
from jax.experimental.pallas import tpu as pltpu

D_MODEL = 1024
EXPAND_WIDTH = 2048
CONV_WIDTH = 3
POOL_WINDOWS = (2, 4, 8, 16)
N_POOL_GROUPS = len(POOL_WINDOWS)
POOL_GROUP_WIDTH = EXPAND_WIDTH // N_POOL_GROUPS
POOL_HIST = max(POOL_WINDOWS) - 1
PAST_LEN = 16384
RMS_EPS = 1e-6
LOG2_E = 1.4426950408889634

LANES = 128
SUBLANES = 8
CONV_CARRY_ROWS = SUBLANES
POOL_CARRY_ROWS = 2 * SUBLANES
CHUNK = POOL_GROUP_WIDTH
N_CHUNKS = EXPAND_WIDTH // CHUNK
PROMPT_TILE = 512
SAMPLE_SPLIT = 2
VMEM_PHYSICAL_BYTES = 64 * 1024 * 1024

BF16 = jnp.bfloat16
F32 = jnp.float32


def _dot(a, b):
    return jnp.dot(a, b, preferred_element_type=F32)


def _rmsnorm(x, g):
    r = lax.rsqrt(jnp.mean(x * x, axis=-1, keepdims=True) + RMS_EPS)
    return x * r * g


def _silu(z):
    return z / (1.0 + jnp.exp2(z * (-LOG2_E)))


def _chunk(j, section=0):
    lo = section * EXPAND_WIDTH + j * CHUNK
    return slice(lo, lo + CHUNK)


def _weight_pieces(src_hbm, dst_ref, piece):
    pr, pc = piece
    rows, cols = src_hbm.shape
    assert rows % pr == 0 and cols % pc == 0
    pairs = []
    for i in range(rows // pr):
        for j in range(cols // pc):
            r, c = pl.ds(i * pr, pr), pl.ds(j * pc, pc)
            pairs.append((src_hbm.at[r, c], dst_ref.at[r, c]))
    return pairs


def _load_weights_as_bf16(weights, raw_refs, sem_ref):
    slots = [r.at[0, k] for r in raw_refs for k in range(r.shape[1])]
    pieces = []
    for src_hbm, dst_ref in weights:
        pieces += _weight_pieces(src_hbm, dst_ref, slots[0].shape)
    n, r = len(pieces), len(slots)

    def copy(k):
        return pltpu.make_async_copy(pieces[k][0], slots[k % r], sem_ref.at[k % r])

    for k in range(min(r, n)):
        copy(k).start()
    for k in range(n):
        copy(k).wait()
        pieces[k][1][...] = slots[k % r][...].astype(BF16)
        if k + r < n:
            copy(k + r).start()


def _time_major(ref, nt):
    return jnp.concatenate([ref[:, t, :] for t in range(nt)], axis=0)


def _row_parts(m, parts):
    return [slice(h * (m // parts), (h + 1) * (m // parts)) for h in range(parts)]


def _conv_gates(hn, win_ref, j, stage, parts=1):
    raw_refs, slot = stage
    raw_ref = raw_refs[j % len(raw_refs)]
    m = hn.shape[0]
    for r in _row_parts(m, parts):
        for sec in range(4):
            raw_ref[slot, sec, r, :] = _dot(hn[r], win_ref[:, _chunk(j, sec)])
    gb, gc, v, z = (raw_ref[slot, sec, 0:m, :] for sec in range(4))
    return gb, gc * v, z


def _conv_prompt_tile(i, tm, x_ref, g_ref, win_ref, cw_ref, cb_ref, wout_ref,
                      x1_ref, nconv_ref, cvh_ref, ybuf_ref, stage, last_tile):
    @pl.when(i == 0)
    def _():
        cvh_ref[...] = jnp.zeros_like(cvh_ref)

    x = x_ref[...]
    hn = _rmsnorm(x, g_ref[0:1, :]).astype(BF16)
    for j in range(N_CHUNKS):
        c = _chunk(j)
        gb, cv, z = _conv_gates(hn, win_ref, j, stage, parts=2 if j == 0 else 1)
        ext = jnp.concatenate([cvh_ref[:, c], cv], axis=0)
        cm1 = pltpu.roll(ext, 1, 0)[CONV_CARRY_ROWS:]
        cm2 = pltpu.roll(ext, 2, 0)[CONV_CARRY_ROWS:]
        conv = cb_ref[:, c] + cm2 * cw_ref[0, 0:1, c]
        conv = conv + cm1 * cw_ref[0, 1:2, c]
        conv = conv + cv * cw_ref[0, 2:3, c]
        ybuf_ref[0:tm, c] = (gb * conv * _silu(z)).astype(BF16)
        cvh_ref[:, c] = cv[tm - CONV_CARRY_ROWS:]
    for r in _row_parts(tm, 2):
        x1_ref[r, :] = x[r] + _dot(ybuf_ref[r, :], wout_ref[...])

    @pl.when(last_tile)
    def _():
        nconv_ref[...] = cvh_ref[CONV_CARRY_ROWS - (CONV_WIDTH - 1):, :]


def _conv_sample_step(nb, nt, xs_ref, sc_ref, g_ref, win_ref, cw_ref, cb_ref, wout_ref,
                      x1s_ref, ncs_ref, ybuf_ref, stage):
    m = nt * nb
    xs = _time_major(xs_ref, nt)
    hn = _rmsnorm(xs, g_ref[0:1, :]).astype(BF16)
    for j in range(N_CHUNKS):
        c = _chunk(j)
        gb, cv, z = _conv_gates(hn, win_ref, j, stage)
        full = jnp.concatenate([sc_ref[:, k, c] for k in range(CONV_WIDTH - 1)] + [cv], axis=0)
        conv = cb_ref[:, c]
        for k in range(CONV_WIDTH):
            conv = conv + full[k * nb:(k + nt) * nb] * cw_ref[0, k:k + 1, c]
        ybuf_ref[0:m, c] = (gb * conv * _silu(z)).astype(BF16)
        for k in range(CONV_WIDTH - 1):
            ncs_ref[:, k, c] = full[(nt + k) * nb:(nt + k + 1) * nb]
    x1 = xs + _dot(ybuf_ref[0:m, :], wout_ref[...])
    for t in range(nt):
        x1s_ref[:, t, :] = x1[t * nb:(t + 1) * nb]


def _conv_layer_kernel(xp_ref, xs_ref, sc_ref, g_ref, win_hbm, cw_ref, cb_ref, wout_hbm, slot_ref,
                       x1p_ref, x1s_ref, ncp_ref, ncs_ref, cvh_ref, ybuf_ref, rawa_ref, rawb_ref,
                       win_ref, wout_ref, wsem_ref,
                       *, tm, tiles_per_seq, n_prompt_steps, nb, nt):
    s = pl.program_id(0)
    raw_refs = (rawa_ref, rawb_ref)
    stage = (raw_refs, slot_ref[0])

    @pl.when(s == 0)
    def _():
        _load_weights_as_bf16([(win_hbm, win_ref), (wout_hbm, wout_ref)], raw_refs, wsem_ref)

    @pl.when(s < n_prompt_steps)
    def _():
        i = s % tiles_per_seq
        _conv_prompt_tile(i, tm, xp_ref, g_ref, win_ref, cw_ref, cb_ref, wout_ref,
                          x1p_ref, ncp_ref, cvh_ref, ybuf_ref, stage, i == tiles_per_seq - 1)

    @pl.when(s >= n_prompt_steps)
    def _():
        _conv_sample_step(nb, nt, xs_ref, sc_ref, g_ref, win_ref, cw_ref, cb_ref, wout_ref,
                          x1s_ref, ncs_ref, ybuf_ref, stage)


def _pool_prompt_tile(i, seq, tm, x1_ref, g_ref, gf_ref, win_ref, wgrp_ref, ps_ref, wout_ref,
                      y_ref, npool_ref, uh_ref, ybuf_ref, stage, last_tile):
    @pl.when(i == 0)
    def _():
        uh_ref[...] = jnp.zeros_like(uh_ref)

    x1 = x1_ref[...]
    hn1 = _rmsnorm(x1, g_ref[1:2, :]).astype(BF16)
    seen = (i * tm + 1 + lax.broadcasted_iota(jnp.int32, (tm, LANES), 0)).astype(F32)
    raw_refs, slot = stage
    for g, w in enumerate(POOL_WINDOWS):
        c = _chunk(g)
        raw_ref = raw_refs[g % len(raw_refs)]
        for r in _row_parts(tm, 2 if g == 0 else 1):
            raw_ref[slot, 0, r, :] = _dot(hn1[r], win_ref[:, _chunk(g, 0)])
            raw_ref[slot, 1, r, :] = _dot(hn1[r], win_ref[:, _chunk(g, 1)])
        u = raw_ref[slot, 0]
        s = jnp.concatenate([uh_ref[:, c], u], axis=0)
        shift = 1
        while shift < w:
            s = s + pltpu.roll(s, shift, 0)
            shift *= 2
        inv = 1.0 / jnp.minimum(jnp.float32(w), seen)
        inv = jnp.concatenate([inv] * (CHUNK // LANES), axis=1)
        p = s[POOL_CARRY_ROWS:] * inv - u
        raw_ref[slot, 2] = _dot(p.astype(BF16), wgrp_ref[_chunk(g), :])
        q, z = raw_ref[slot, 2], raw_ref[slot, 1]
        ybuf_ref[0:tm, c] = (q * ps_ref[:, c] * _silu(z)).astype(BF16)
        uh_ref[:, c] = u[tm - POOL_CARRY_ROWS:]
    kh = EXPAND_WIDTH // 2
    rows = _row_parts(tm, 2)
    lo = [_dot(ybuf_ref[r, 0:kh], wout_ref[0:kh, :]) for r in rows]
    hi = [_dot(ybuf_ref[r, kh:], wout_ref[kh:, :]) for r in rows]
    for r, a, b in zip(rows, lo, hi):
        y_ref[r, :] = _rmsnorm(x1_ref[r, :] + (a + b), gf_ref[...])

    @pl.when(last_tile)
    def _():
        for b in range(npool_ref.shape[1]):
            @pl.when(seq == b)
            def _(b=b):
                npool_ref[:, b, :] = uh_ref[POOL_CARRY_ROWS - POOL_HIST:, :]


def _pool_sample_step(g, nb, nt, start_pos, x1s_ref, sp_ref, g_ref, gf_ref, win_ref, wgrp_ref, ps_ref,
                      wout_ref, ys_ref, nps_ref, acc_ref, hn_ref, stage):
    w = POOL_WINDOWS[g]
    c = _chunk(g)
    m = nt * nb
    if g == 0:
        x1 = _time_major(x1s_ref, nt)
        acc_ref[...] = x1
        hn_ref[...] = _rmsnorm(x1, g_ref[1:2, :]).astype(BF16)
    hn1 = hn_ref[...]
    raw_refs, slot = stage
    raw_ref = raw_refs[g % len(raw_refs)]
    raw_ref[slot, 0, 0:m, :] = _dot(hn1, win_ref[:, _chunk(g, 0)])
    raw_ref[slot, 1, 0:m, :] = _dot(hn1, win_ref[:, _chunk(g, 1)])
    u = raw_ref[slot, 0, 0:m, :]

    def full(r):
        if r < POOL_HIST:
            return sp_ref[r]
        return u[(r - POOL_HIST) * nb:(r - POOL_HIST + 1) * nb]

    parts = []
    for t in range(nt):
        win = full(POOL_HIST + t)
        for k in range(1, w):
            win = win + full(POOL_HIST + t - k)
        inv = 1.0 / float(min(w, start_pos + t + 1))
        parts.append(win * inv - full(POOL_HIST + t))
    p = jnp.concatenate(parts, axis=0).astype(BF16)
    raw_ref[slot, 2, 0:m, :] = _dot(p, wgrp_ref[_chunk(g), :])
    q, z = raw_ref[slot, 2, 0:m, :], raw_ref[slot, 1, 0:m, :]
    y = (q * ps_ref[:, c] * _silu(z)).astype(BF16)
    acc_ref[...] += _dot(y, wout_ref[c, :])

    for r in range(POOL_HIST):
        nps_ref[r] = full(r + nt)
    if g == N_POOL_GROUPS - 1:
        ys = _rmsnorm(acc_ref[...], gf_ref[...])
        for t in range(nt):
            ys_ref[:, t, :] = ys[t * nb:(t + 1) * nb]


def _pool_layer_kernel(x1p_ref, x1s_ref, sp_ref, g_ref, gf_ref, win_hbm, wgrp_hbm, ps_ref, wout_hbm, slot_ref,
                       yp_ref, ys_ref, npp_ref, nps_ref, uh_ref, ybuf_ref, acc_ref, hn_ref,
                       rawa_ref, rawb_ref, win_ref, wgrp_ref, wout_ref, wsem_ref,
                       *, tm, tiles_per_seq, n_prompt_steps, nb, nt, start_pos):
    s = pl.program_id(0)
    raw_refs = (rawa_ref, rawb_ref)
    stage = (raw_refs, slot_ref[0])
    phase = jnp.maximum(s - n_prompt_steps, 0) % N_POOL_GROUPS

    @pl.when(s == 0)
    def _():
        _load_weights_as_bf16([(win_hbm, win_ref), (wgrp_hbm, wgrp_ref), (wout_hbm, wout_ref)],
                              raw_refs, wsem_ref)

    @pl.when(s < n_prompt_steps)
    def _():
        i = s % tiles_per_seq
        _pool_prompt_tile(i, s // tiles_per_seq, tm, x1p_ref, g_ref, gf_ref, win_ref, wgrp_ref, ps_ref, wout_ref,
                          yp_ref, npp_ref, uh_ref, ybuf_ref, stage, i == tiles_per_seq - 1)

    for g in range(N_POOL_GROUPS):
        @pl.when(jnp.logical_and(s >= n_prompt_steps, phase == g))
        def _(g=g):
            _pool_sample_step(g, nb, nt, start_pos, x1s_ref, sp_ref, g_ref, gf_ref, win_ref, wgrp_ref,
                              ps_ref, wout_ref, ys_ref, nps_ref, acc_ref, hn_ref, stage)


def _resident(shape):
    zeros = (0,) * len(shape)
    return pl.BlockSpec(shape, lambda s: zeros, pipeline_mode=pl.Buffered(1))


def _layer_calls(xp, xs, sc, sp, g, gf, win0, cw, cb, wout0, win1, wgrp, ps, wout1):
    B, S, D = xp.shape
    NB, NT, _ = xs.shape
    E = EXPAND_WIDTH
    tm = PROMPT_TILE
    assert S % tm == 0 and tm >= POOL_CARRY_ROWS and NB % SAMPLE_SPLIT == 0
    tiles = S // tm
    n_prompt = B * tiles
    nb = NB // SAMPLE_SPLIT
    m = NT * nb
    assert nb % SUBLANES == 0 and m <= tm

    def prompt_tile(s):
        sc_ = jnp.minimum(s, n_prompt - 1)
        return (sc_ // tiles, sc_ % tiles, 0)

    def prompt_seq(s):
        return (jnp.minimum(s, n_prompt - 1) // tiles, 0, 0)

    def conv_part(s):
        return (jnp.maximum(s - n_prompt, 0), 0, 0)

    def pool_part(s):
        return (jnp.maximum(s - n_prompt, 0) // N_POOL_GROUPS, 0, 0)

    def pool_group(s):
        q = jnp.maximum(s - n_prompt, 0)
        return (0, q // N_POOL_GROUPS, q % N_POOL_GROUPS)

    tile_bytes = 4 * tm * D * 4
    temp_bytes = 16 * tm * CHUNK * 4 + tm * E * 2
    stage_bytes = 2 * 4 * tm * CHUNK * 4
    headroom = 4 << 20
    stage_slot = jnp.zeros((1,), jnp.int32)
    hbm = pl.BlockSpec(memory_space=pl.ANY)
    conv_planes, pool_planes = 4, 3

    conv_sample_bytes = 2 * 4 * (2 * NT * nb * D + 2 * (CONV_WIDTH - 1) * nb * E)
    conv_limit = min(2 * (win0.size + wout0.size) + tile_bytes + temp_bytes + stage_bytes
                     + conv_sample_bytes + headroom, VMEM_PHYSICAL_BYTES - headroom)
    x1p, x1s, ncp, ncs = pl.pallas_call(
        functools.partial(_conv_layer_kernel, tm=tm, tiles_per_seq=tiles, n_prompt_steps=n_prompt,
                          nb=nb, nt=NT),
        grid=(n_prompt + SAMPLE_SPLIT,),
        in_specs=[
            pl.BlockSpec((None, tm, D), prompt_tile),
            pl.BlockSpec((nb, NT, D), conv_part),
            pl.BlockSpec((nb, CONV_WIDTH - 1, E), conv_part),
            _resident(g.shape), hbm, _resident(cw.shape), _resident(cb.shape), hbm,
            pl.BlockSpec(memory_space=pltpu.SMEM),
        ],
        out_specs=[
            pl.BlockSpec((None, tm, D), prompt_tile),
            pl.BlockSpec((nb, NT, D), conv_part),
            pl.BlockSpec((None, CONV_WIDTH - 1, E), prompt_seq),
            pl.BlockSpec((nb, CONV_WIDTH - 1, E), conv_part),
        ],
        out_shape=[
            jax.ShapeDtypeStruct((B, S, D), F32),
            jax.ShapeDtypeStruct((NB, NT, D), F32),
            jax.ShapeDtypeStruct((B, CONV_WIDTH - 1, E), F32),
            jax.ShapeDtypeStruct((NB, CONV_WIDTH - 1, E), F32),
        ],
        scratch_shapes=[
            pltpu.VMEM((CONV_CARRY_ROWS, E), F32),
            pltpu.VMEM((tm, E), BF16),
            pltpu.VMEM((1, conv_planes, tm, CHUNK), F32),
            pltpu.VMEM((1, conv_planes, tm, CHUNK), F32),
            pltpu.VMEM(win0.shape, BF16),
            pltpu.VMEM(wout0.shape, BF16),
            pltpu.SemaphoreType.DMA((2 * conv_planes,)),
        ],
        compiler_params=pltpu.CompilerParams(
            dimension_semantics=("arbitrary",), vmem_limit_bytes=conv_limit),
        name="conv_layer",
    )(xp, xs, sc, g, win0, cw, cb, wout0, stage_slot)

    pool_sample_bytes = 2 * 4 * (2 * NT * nb * D + 2 * nb * (POOL_HIST + 1) * CHUNK) + m * D * 6
    pool_limit = min(2 * (win1.size + wgrp.size + wout1.size) + tile_bytes + temp_bytes
                     + stage_bytes + pool_sample_bytes + headroom, VMEM_PHYSICAL_BYTES - headroom)
    yp, ys, npp, nps = pl.pallas_call(
        functools.partial(_pool_layer_kernel, tm=tm, tiles_per_seq=tiles, n_prompt_steps=n_prompt,
                          nb=nb, nt=NT, start_pos=PAST_LEN),
        grid=(n_prompt + SAMPLE_SPLIT * N_POOL_GROUPS,),
        in_specs=[
            pl.BlockSpec((None, tm, D), prompt_tile),
            pl.BlockSpec((nb, NT, D), pool_part),
            pl.BlockSpec((POOL_HIST, nb, CHUNK), pool_group),
            _resident(g.shape), _resident(gf.shape), hbm, hbm, _resident(ps.shape), hbm,
            pl.BlockSpec(memory_space=pltpu.SMEM),
        ],
        out_specs=[
            pl.BlockSpec((None, tm, D), prompt_tile),
            pl.BlockSpec((nb, NT, D), pool_part),
            pl.BlockSpec((POOL_HIST, B, E), lambda s: (0, 0, 0)),
            pl.BlockSpec((POOL_HIST, nb, CHUNK), pool_group),
        ],
        out_shape=[
            jax.ShapeDtypeStruct((B, S, D), F32),
            jax.ShapeDtypeStruct((NB, NT, D), F32),
            jax.ShapeDtypeStruct((POOL_HIST, B, E), F32),
            jax.ShapeDtypeStruct((POOL_HIST, NB, E), F32),
        ],
        scratch_shapes=[
            pltpu.VMEM((POOL_CARRY_ROWS, E), F32),
            pltpu.VMEM((tm, E), BF16),
            pltpu.VMEM((m, D), F32),
            pltpu.VMEM((m, D), BF16),
            pltpu.VMEM((1, pool_planes, tm, CHUNK), F32),
            pltpu.VMEM((1, pool_planes, tm, CHUNK), F32),
            pltpu.VMEM(win1.shape, BF16),
            pltpu.VMEM(wgrp.shape, BF16),
            pltpu.VMEM(wout1.shape, BF16),
            pltpu.SemaphoreType.DMA((2 * pool_planes,)),
        ],
        compiler_params=pltpu.CompilerParams(
            dimension_semantics=("arbitrary",), vmem_limit_bytes=pool_limit),
        name="pool_layer",
    )(x1p, x1s, sp, g, gf, win1, wgrp, ps, wout1, stage_slot)
    return yp, ys, ncp, ncs, npp, nps


def kernel(x_prompt, x_sample, state_conv, state_pool, norm_g, final_norm_g,
           conv_w_in, conv_w, conv_b, conv_w_out,
           pool_w_in, pool_w_grp, pool_scale, pool_w_out):
    assert norm_g.shape[0] == 2 and conv_w_in.shape[0] == 1 and pool_w_in.shape[0] == 1
    gf = final_norm_g.reshape(1, D_MODEL)
    win0, wout0, win1, wout1 = conv_w_in[0], conv_w_out[0], pool_w_in[0], pool_w_out[0]
    wgrp = pool_w_grp[0].reshape(EXPAND_WIDTH, POOL_GROUP_WIDTH)
    yp, ys, ncp, ncs, npp, nps = _layer_calls(
        x_prompt, x_sample, state_conv[0], jnp.transpose(state_pool[0], (1, 0, 2)), norm_g, gf,
        win0, conv_w, conv_b, wout0, win1, wgrp, pool_scale, wout1)
    return (yp, ys, ncp[None], ncs[None],
            jnp.transpose(npp, (1, 0, 2))[None], jnp.transpose(nps, (1, 0, 2))[None])
```

```python
import functools

import jax
import jax.numpy as jnp
from jax import lax
from jax.experimental import pallas as pl
from jax.experimental.pallas import tpu as pltpu

D_MODEL = 1024
EXPAND_WIDTH = 2048
CONV_WIDTH = 3
POOL_WINDOWS = (2, 4, 8, 16)
N_POOL_GROUPS = len(POOL_WINDOWS)
POOL_GROUP_WIDTH = EXPAND_WIDTH // N_POOL_GROUPS
POOL_HIST = max(POOL_WINDOWS) - 1
PAST_LEN = 16384
RMS_EPS = 1e-6
LOG2_E = 1.4426950408889634

LANES = 128
SUBLANES = 8
CONV_CARRY_ROWS = SUBLANES
POOL_CARRY_ROWS = 2 * SUBLANES
CHUNK = POOL_GROUP_WIDTH
N_CHUNKS = EXPAND_WIDTH // CHUNK
PROMPT_TILE = 512
SAMPLE_SPLIT = 2
VMEM_PHYSICAL_BYTES = 64 * 1024 * 1024

BF16 = jnp.bfloat16
F32 = jnp.float32


def _dot(a, b):
    return jnp.dot(a, b, preferred_element_type=F32)


def _rmsnorm(x, g):
    r = lax.rsqrt(jnp.mean(x * x, axis=-1, keepdims=True) + RMS_EPS)
    return x * r * g


def _silu(z):
    return z / (1.0 + jnp.exp2(z * (-LOG2_E)))


def _chunk(j, section=0):
    lo = section * EXPAND_WIDTH + j * CHUNK
    return slice(lo, lo + CHUNK)


def _weight_pieces(src_hbm, dst_ref, piece):
    pr, pc = piece
    rows, cols = src_hbm.shape
    assert rows % pr == 0 and cols % pc == 0
    pairs = []
    for i in range(rows // pr):
        for j in range(cols // pc):
            r, c = pl.ds(i * pr, pr), pl.ds(j * pc, pc)
            pairs.append((src_hbm.at[r, c], dst_ref.at[r, c]))
    return pairs


def _load_weights_as_bf16(weights, raw_refs, sem_ref):
    slots = [r.at[0, k] for r in raw_refs for k in range(r.shape[1])]
    pieces = []
    for src_hbm, dst_ref in weights:
        pieces += _weight_pieces(src_hbm, dst_ref, slots[0].shape)
    n, r = len(pieces), len(slots)

    def copy(k):
        return pltpu.make_async_copy(pieces[k][0], slots[k % r], sem_ref.at[k % r])

    for k in range(min(r, n)):
        copy(k).start()
    for k in range(n):
        copy(k).wait()
        pieces[k][1][...] = slots[k % r][...].astype(BF16)
        if k + r < n:
            copy(k + r).start()


def _time_major(ref, nt):
    return jnp.concatenate([ref[:, t, :] for t in range(nt)], axis=0)


def _row_parts(m, parts):
    return [slice(h * (m // parts), (h + 1) * (m // parts)) for h in range(parts)]


def _conv_gates(hn, win_ref, j, stage, parts=1):
    raw_refs, slot = stage
    raw_ref = raw_refs[j % len(raw_refs)]
    m = hn.shape[0]
    for r in _row_parts(m, parts):
        for sec in range(4):
            raw_ref[slot, sec, r, :] = _dot(hn[r], win_ref[:, _chunk(j, sec)])
    gb, gc, v, z = (raw_ref[slot, sec, 0:m, :] for sec in range(4))
    return gb, gc * v, z


def _conv_prompt_tile(i, tm, x_ref, g_ref, win_ref, cw_ref, cb_ref, wout_ref,
                      x1_ref, nconv_ref, cvh_ref, ybuf_ref, stage, last_tile):
    @pl.when(i == 0)
    def _():
        cvh_ref[...] = jnp.zeros_like(cvh_ref)

    x = x_ref[...]
    hn = _rmsnorm(x, g_ref[0:1, :]).astype(BF16)
    for j in range(N_CHUNKS):
        c = _chunk(j)
        gb, cv, z = _conv_gates(hn, win_ref, j, stage, parts=2 if j == 0 else 1)
        ext = jnp.concatenate([cvh_ref[:, c], cv], axis=0)
        cm1 = pltpu.roll(ext, 1, 0)[CONV_CARRY_ROWS:]
        cm2 = pltpu.roll(ext, 2, 0)[CONV_CARRY_ROWS:]
        conv = cb_ref[:, c] + cm2 * cw_ref[0, 0:1, c]
        conv = conv + cm1 * cw_ref[0, 1:2, c]
        conv = conv + cv * cw_ref[0, 2:3, c]
        ybuf_ref[0:tm, c] = (gb * conv * _silu(z)).astype(BF16)
        cvh_ref[:, c] = cv[tm - CONV_CARRY_ROWS:]
    for r in _row_parts(tm, 2):
        x1_ref[r, :] = x[r] + _dot(ybuf_ref[r, :], wout_ref[...])

    @pl.when(last_tile)
    def _():
        nconv_ref[...] = cvh_ref[CONV_CARRY_ROWS - (CONV_WIDTH - 1):, :]


def _conv_sample_step(nb, nt, xs_ref, sc_ref, g_ref, win_ref, cw_ref, cb_ref, wout_ref,
                      x1s_ref, ncs_ref, ybuf_ref, stage):
    m = nt * nb
    xs = _time_major(xs_ref, nt)
    hn = _rmsnorm(xs, g_ref[0:1, :]).astype(BF16)
    for j in range(N_CHUNKS):
        c = _chunk(j)
        gb, cv, z = _conv_gates(hn, win_ref, j, stage)
        full = jnp.concatenate([sc_ref[:, k, c] for k in range(CONV_WIDTH - 1)] + [cv], axis=0)
        conv = cb_ref[:, c]
        for k in range(CONV_WIDTH):
            conv = conv + full[k * nb:(k + nt) * nb] * cw_ref[0, k:k + 1, c]
        ybuf_ref[0:m, c] = (gb * conv * _silu(z)).astype(BF16)
        for k in range(CONV_WIDTH - 1):
            ncs_ref[:, k, c] = full[(nt + k) * nb:(nt + k + 1) * nb]
    x1 = xs + _dot(ybuf_ref[0:m, :], wout_ref[...])
    for t in range(nt):
        x1s_ref[:, t, :] = x1[t * nb:(t + 1) * nb]


def _conv_layer_kernel(xp_ref, xs_ref, sc_ref, g_ref, win_hbm, cw_ref, cb_ref, wout_hbm, slot_ref,
                       x1p_ref, x1s_ref, ncp_ref, ncs_ref, cvh_ref, ybuf_ref, rawa_ref, rawb_ref,
                       win_ref, wout_ref, wsem_ref,
                       *, tm, tiles_per_seq, n_prompt_steps, nb, nt):
    s = pl.program_id(0)
    raw_refs = (rawa_ref, rawb_ref)
    stage = (raw_refs, slot_ref[0])

    @pl.when(s == 0)
    def _():
        _load_weights_as_bf16([(win_hbm, win_ref), (wout_hbm, wout_ref)], raw_refs, wsem_ref)

    @pl.when(s < n_prompt_steps)
    def _():
        i = s % tiles_per_seq
        _conv_prompt_tile(i, tm, xp_ref, g_ref, win_ref, cw_ref, cb_ref, wout_ref,
                          x1p_ref, ncp_ref, cvh_ref, ybuf_ref, stage, i == tiles_per_seq - 1)

    @pl.when(s >= n_prompt_steps)
    def _():
        _conv_sample_step(nb, nt, xs_ref, sc_ref, g_ref, win_ref, cw_ref, cb_ref, wout_ref,
                          x1s_ref, ncs_ref, ybuf_ref, stage)


def _pool_prompt_tile(i, seq, tm, x1_ref, g_ref, gf_ref, win_ref, wug_ref, ps_ref, wout_ref,
                      y_ref, npool_ref, vh_ref, ybuf_ref, stage, last_tile):
    @pl.when(i == 0)
    def _():
        vh_ref[...] = jnp.zeros_like(vh_ref)

    x1 = x1_ref[...]
    hn1 = _rmsnorm(x1, g_ref[1:2, :]).astype(BF16)
    seen = (i * tm + 1 + lax.broadcasted_iota(jnp.int32, (tm, LANES), 0)).astype(F32)
    raw_refs, slot = stage
    for g, w in enumerate(POOL_WINDOWS):
        c = _chunk(g)
        raw_ref = raw_refs[g % len(raw_refs)]
        for r in _row_parts(tm, 2 if g == 0 else 1):
            raw_ref[slot, 0, r, :] = _dot(hn1[r], wug_ref[:, c])
            raw_ref[slot, 1, r, :] = _dot(hn1[r], win_ref[:, _chunk(g, 1)])
        v = raw_ref[slot, 0]
        s = jnp.concatenate([vh_ref[:, c], v], axis=0)
        shift = 1
        while shift < w:
            s = s + pltpu.roll(s, shift, 0)
            shift *= 2
        inv = 1.0 / jnp.minimum(jnp.float32(w), seen)
        inv = jnp.concatenate([inv] * (CHUNK // LANES), axis=1)
        q = s[POOL_CARRY_ROWS:] * inv - v
        z = raw_ref[slot, 1]
        ybuf_ref[0:tm, c] = (q * ps_ref[:, c] * _silu(z)).astype(BF16)
        vh_ref[:, c] = v[tm - POOL_CARRY_ROWS:]
    kh = EXPAND_WIDTH // 2
    rows = _row_parts(tm, 2)
    lo = [_dot(ybuf_ref[r, 0:kh], wout_ref[0:kh, :]) for r in rows]
    hi = [_dot(ybuf_ref[r, kh:], wout_ref[kh:, :]) for r in rows]
    for r, a, b in zip(rows, lo, hi):
        y_ref[r, :] = _rmsnorm(x1_ref[r, :] + (a + b), gf_ref[...])

    @pl.when(last_tile)
    def _():
        tail = slice(tm - POOL_CARRY_ROWS, tm)
        hn_tail = _rmsnorm(x1_ref[tail, :], g_ref[1:2, :]).astype(BF16)
        u_tail = _dot(hn_tail, win_ref[:, 0:EXPAND_WIDTH])
        for b in range(npool_ref.shape[1]):
            @pl.when(seq == b)
            def _(b=b):
                npool_ref[:, b, :] = u_tail[POOL_CARRY_ROWS - POOL_HIST:, :]


def _pool_sample_step(g, nb, nt, start_pos, x1s_ref, sp_ref, g_ref, gf_ref, win_ref, wgrp_ref, ps_ref,
                      wout_ref, ys_ref, nps_ref, acc_ref, hn_ref, stage):
    w = POOL_WINDOWS[g]
    c = _chunk(g)
    m = nt * nb
    if g == 0:
        x1 = _time_major(x1s_ref, nt)
        acc_ref[...] = x1
        hn_ref[...] = _rmsnorm(x1, g_ref[1:2, :]).astype(BF16)
    hn1 = hn_ref[...]
    raw_refs, slot = stage
    raw_ref = raw_refs[g % len(raw_refs)]
    raw_ref[slot, 0, 0:m, :] = _dot(hn1, win_ref[:, _chunk(g, 0)])
    raw_ref[slot, 1, 0:m, :] = _dot(hn1, win_ref[:, _chunk(g, 1)])
    u = raw_ref[slot, 0, 0:m, :]

    def full(r):
        if r < POOL_HIST:
            return sp_ref[r]
        return u[(r - POOL_HIST) * nb:(r - POOL_HIST + 1) * nb]

    parts = []
    for t in range(nt):
        win = full(POOL_HIST + t)
        for k in range(1, w):
            win = win + full(POOL_HIST + t - k)
        inv = 1.0 / float(min(w, start_pos + t + 1))
        parts.append(win * inv - full(POOL_HIST + t))
    p = jnp.concatenate(parts, axis=0).astype(BF16)
    raw_ref[slot, 2, 0:m, :] = _dot(p, wgrp_ref[_chunk(g), :])
    q, z = raw_ref[slot, 2, 0:m, :], raw_ref[slot, 1, 0:m, :]
    y = (q * ps_ref[:, c] * _silu(z)).astype(BF16)
    acc_ref[...] += _dot(y, wout_ref[c, :])

    for r in range(POOL_HIST):
        nps_ref[r] = full(r + nt)
    if g == N_POOL_GROUPS - 1:
        ys = _rmsnorm(acc_ref[...], gf_ref[...])
        for t in range(nt):
            ys_ref[:, t, :] = ys[t * nb:(t + 1) * nb]


def _pool_layer_kernel(x1p_ref, x1s_ref, sp_ref, g_ref, gf_ref, win_hbm, wgrp_hbm, ps_ref, wout_hbm, slot_ref,
                       yp_ref, ys_ref, npp_ref, nps_ref, vh_ref, ybuf_ref, acc_ref, hn_ref,
                       rawa_ref, rawb_ref, win_ref, wgrp_ref, wout_ref, wug_ref, wsem_ref,
                       *, tm, tiles_per_seq, n_prompt_steps, nb, nt, start_pos):
    s = pl.program_id(0)
    raw_refs = (rawa_ref, rawb_ref)
    stage = (raw_refs, slot_ref[0])
    phase = jnp.maximum(s - n_prompt_steps, 0) % N_POOL_GROUPS

    @pl.when(s == 0)
    def _():
        _load_weights_as_bf16([(win_hbm, win_ref), (wgrp_hbm, wgrp_ref), (wout_hbm, wout_ref)],
                              raw_refs, wsem_ref)
        for g in range(N_POOL_GROUPS):
            c = _chunk(g)
            wug_ref[:, c] = _dot(win_ref[:, c], wgrp_ref[c, :]).astype(BF16)

    @pl.when(s < n_prompt_steps)
    def _():
        i = s % tiles_per_seq
        _pool_prompt_tile(i, s // tiles_per_seq, tm, x1p_ref, g_ref, gf_ref, win_ref, wug_ref, ps_ref, wout_ref,
                          yp_ref, npp_ref, vh_ref, ybuf_ref, stage, i == tiles_per_seq - 1)

    for g in range(N_POOL_GROUPS):
        @pl.when(jnp.logical_and(s >= n_prompt_steps, phase == g))
        def _(g=g):
            _pool_sample_step(g, nb, nt, start_pos, x1s_ref, sp_ref, g_ref, gf_ref, win_ref, wgrp_ref,
                              ps_ref, wout_ref, ys_ref, nps_ref, acc_ref, hn_ref, stage)


def _resident(shape):
    zeros = (0,) * len(shape)
    return pl.BlockSpec(shape, lambda s: zeros, pipeline_mode=pl.Buffered(1))


def _layer_calls(xp, xs, sc, sp, g, gf, win0, cw, cb, wout0, win1, wgrp, ps, wout1):
    B, S, D = xp.shape
    NB, NT, _ = xs.shape
    E = EXPAND_WIDTH
    tm = PROMPT_TILE
    assert S % tm == 0 and tm >= POOL_CARRY_ROWS and NB % SAMPLE_SPLIT == 0
    tiles = S // tm
    n_prompt = B * tiles
    nb = NB // SAMPLE_SPLIT
    m = NT * nb
    assert nb % SUBLANES == 0 and m <= tm

    def prompt_tile(s):
        sc_ = jnp.minimum(s, n_prompt - 1)
        return (sc_ // tiles, sc_ % tiles, 0)

    def prompt_seq(s):
        return (jnp.minimum(s, n_prompt - 1) // tiles, 0, 0)

    def conv_part(s):
        return (jnp.maximum(s - n_prompt, 0), 0, 0)

    def pool_part(s):
        return (jnp.maximum(s - n_prompt, 0) // N_POOL_GROUPS, 0, 0)

    def pool_group(s):
        q = jnp.maximum(s - n_prompt, 0)
        return (0, q // N_POOL_GROUPS, q % N_POOL_GROUPS)

    tile_bytes = 4 * tm * D * 4
    temp_bytes = 16 * tm * CHUNK * 4 + tm * E * 2
    stage_bytes = 2 * 4 * tm * CHUNK * 4
    headroom = 4 << 20
    stage_slot = jnp.zeros((1,), jnp.int32)
    hbm = pl.BlockSpec(memory_space=pl.ANY)
    conv_planes, pool_planes = 4, 3

    conv_sample_bytes = 2 * 4 * (2 * NT * nb * D + 2 * (CONV_WIDTH - 1) * nb * E)
    conv_limit = min(2 * (win0.size + wout0.size) + tile_bytes + temp_bytes + stage_bytes
                     + conv_sample_bytes + headroom, VMEM_PHYSICAL_BYTES - headroom)
    x1p, x1s, ncp, ncs = pl.pallas_call(
        functools.partial(_conv_layer_kernel, tm=tm, tiles_per_seq=tiles, n_prompt_steps=n_prompt,
                          nb=nb, nt=NT),
        grid=(n_prompt + SAMPLE_SPLIT,),
        in_specs=[
            pl.BlockSpec((None, tm, D), prompt_tile),
            pl.BlockSpec((nb, NT, D), conv_part),
            pl.BlockSpec((nb, CONV_WIDTH - 1, E), conv_part),
            _resident(g.shape), hbm, _resident(cw.shape), _resident(cb.shape), hbm,
            pl.BlockSpec(memory_space=pltpu.SMEM),
        ],
        out_specs=[
            pl.BlockSpec((None, tm, D), prompt_tile),
            pl.BlockSpec((nb, NT, D), conv_part),
            pl.BlockSpec((None, CONV_WIDTH - 1, E), prompt_seq),
            pl.BlockSpec((nb, CONV_WIDTH - 1, E), conv_part),
        ],
        out_shape=[
            jax.ShapeDtypeStruct((B, S, D), F32),
            jax.ShapeDtypeStruct((NB, NT, D), F32),
            jax.ShapeDtypeStruct((B, CONV_WIDTH - 1, E), F32),
            jax.ShapeDtypeStruct((NB, CONV_WIDTH - 1, E), F32),
        ],
        scratch_shapes=[
            pltpu.VMEM((CONV_CARRY_ROWS, E), F32),
            pltpu.VMEM((tm, E), BF16),
            pltpu.VMEM((1, conv_planes, tm, CHUNK), F32),
            pltpu.VMEM((1, conv_planes, tm, CHUNK), F32),
            pltpu.VMEM(win0.shape, BF16),
            pltpu.VMEM(wout0.shape, BF16),
            pltpu.SemaphoreType.DMA((2 * conv_planes,)),
        ],
        compiler_params=pltpu.CompilerParams(
            dimension_semantics=("arbitrary",), vmem_limit_bytes=conv_limit),
        name="conv_layer",
    )(xp, xs, sc, g, win0, cw, cb, wout0, stage_slot)

    pool_sample_bytes = 2 * 4 * (2 * NT * nb * D + 2 * nb * (POOL_HIST + 1) * CHUNK) + m * D * 6
    pool_limit = min(2 * (win1.size + wgrp.size + wout1.size + D * E) + tile_bytes + temp_bytes
                     + stage_bytes + pool_sample_bytes + headroom, VMEM_PHYSICAL_BYTES - headroom)
    yp, ys, npp, nps = pl.pallas_call(
        functools.partial(_pool_layer_kernel, tm=tm, tiles_per_seq=tiles, n_prompt_steps=n_prompt,
                          nb=nb, nt=NT, start_pos=PAST_LEN),
        grid=(n_prompt + SAMPLE_SPLIT * N_POOL_GROUPS,),
        in_specs=[
            pl.BlockSpec((None, tm, D), prompt_tile),
            pl.BlockSpec((nb, NT, D), pool_part),
            pl.BlockSpec((POOL_HIST, nb, CHUNK), pool_group),
            _resident(g.shape), _resident(gf.shape), hbm, hbm, _resident(ps.shape), hbm,
            pl.BlockSpec(memory_space=pltpu.SMEM),
        ],
        out_specs=[
            pl.BlockSpec((None, tm, D), prompt_tile),
            pl.BlockSpec((nb, NT, D), pool_part),
            pl.BlockSpec((POOL_HIST, B, E), lambda s: (0, 0, 0)),
            pl.BlockSpec((POOL_HIST, nb, CHUNK), pool_group),
        ],
        out_shape=[
            jax.ShapeDtypeStruct((B, S, D), F32),
            jax.ShapeDtypeStruct((NB, NT, D), F32),
            jax.ShapeDtypeStruct((POOL_HIST, B, E), F32),
            jax.ShapeDtypeStruct((POOL_HIST, NB, E), F32),
        ],
        scratch_shapes=[
            pltpu.VMEM((POOL_CARRY_ROWS, E), F32),
            pltpu.VMEM((tm, E), BF16),
            pltpu.VMEM((m, D), F32),
            pltpu.VMEM((m, D), BF16),
            pltpu.VMEM((1, pool_planes, tm, CHUNK), F32),
            pltpu.VMEM((1, pool_planes, tm, CHUNK), F32),
            pltpu.VMEM(win1.shape, BF16),
            pltpu.VMEM(wgrp.shape, BF16),
            pltpu.VMEM(wout1.shape, BF16),
            pltpu.VMEM((D, E), BF16),
            pltpu.SemaphoreType.DMA((2 * pool_planes,)),
        ],
        compiler_params=pltpu.CompilerParams(
            dimension_semantics=("arbitrary",), vmem_limit_bytes=pool_limit),
        name="pool_layer",
    )(x1p, x1s, sp, g, gf, win1, wgrp, ps, wout1, stage_slot)
    return yp, ys, ncp, ncs, npp, nps


def kernel(x_prompt, x_sample, state_conv, state_pool, norm_g, final_norm_g,
           conv_w_in, conv_w, conv_b, conv_w_out,
           pool_w_in, pool_w_grp, pool_scale, pool_w_out):
    assert norm_g.shape[0] == 2 and conv_w_in.shape[0] == 1 and pool_w_in.shape[0] == 1
    gf = final_norm_g.reshape(1, D_MODEL)
    win0, wout0, win1, wout1 = conv_w_in[0], conv_w_out[0], pool_w_in[0], pool_w_out[0]
    wgrp = pool_w_grp[0].reshape(EXPAND_WIDTH, POOL_GROUP_WIDTH)
    yp, ys, ncp, ncs, npp, nps = _layer_calls(
        x_prompt, x_sample, state_conv[0], jnp.transpose(state_pool[0], (1, 0, 2)), norm_g, gf,
        win0, conv_w, conv_b, wout0, win1, wgrp, pool_scale, wout1)
    return (yp, ys, ncp[None], ncs[None],
            jnp.transpose(npp, (1, 0, 2))[None], jnp.transpose(nps, (1, 0, 2))[None])
```

```python
import functools

import jax
import jax.numpy as jnp
from jax import lax
from jax.experimental import pallas as pl
from jax.experimental.pallas import tpu as pltpu

D_MODEL = 1024
EXPAND_WIDTH = 2048
CONV_WIDTH = 3
POOL_WINDOWS = (2, 4, 8, 16)
N_POOL_GROUPS = len(POOL_WINDOWS)
POOL_GROUP_WIDTH = EXPAND_WIDTH // N_POOL_GROUPS
POOL_HIST = max(POOL_WINDOWS) - 1
PAST_LEN = 16384
RMS_EPS = 1e-6
LOG2_E = 1.4426950408889634

LANES = 128
SUBLANES = 8
CONV_CARRY_ROWS = SUBLANES
POOL_CARRY_ROWS = 2 * SUBLANES
CHUNK = POOL_GROUP_WIDTH
N_CHUNKS = EXPAND_WIDTH // CHUNK
PROMPT_TILE = 512
SAMPLE_SPLIT = 2
VMEM_PHYSICAL_BYTES = 64 * 1024 * 1024

BF16 = jnp.bfloat16
F32 = jnp.float32


def _dot(a, b):
    return jnp.dot(a, b, preferred_element_type=F32)


def _rmsnorm(x, g):
    r = lax.rsqrt(jnp.mean(x * x, axis=-1, keepdims=True) + RMS_EPS)
    return x * r * g


def _silu(z):
    return z / (1.0 + jnp.exp2(z * (-LOG2_E)))


def _chunk(j, section=0):
    lo = section * EXPAND_WIDTH + j * CHUNK
    return slice(lo, lo + CHUNK)


def _weight_pieces(src_hbm, dst_ref, piece):
    pr, pc = piece
    rows, cols = src_hbm.shape
    assert rows % pr == 0 and cols % pc == 0
    pairs = []
    for i in range(rows // pr):
        for j in range(cols // pc):
            r, c = pl.ds(i * pr, pr), pl.ds(j * pc, pc)
            pairs.append((src_hbm.at[r, c], dst_ref.at[r, c]))
    return pairs


def _load_weights_as_bf16(weights, raw_refs, sem_ref, first=None, then=None):
    slots = [r.at[0, k] for r in raw_refs for k in range(r.shape[1])]
    pr, pc = slots[0].shape
    early, late = [], []
    for w, (src_hbm, dst_ref) in enumerate(weights):
        cols = src_hbm.shape[1] // pc
        for k, piece in enumerate(_weight_pieces(src_hbm, dst_ref, (pr, pc))):
            (early if first is not None and first(w, k // cols, k % cols) else late).append(piece)
    pieces = early + late
    n, r = len(pieces), len(slots)

    def copy(k):
        return pltpu.make_async_copy(pieces[k][0], slots[k % r], sem_ref.at[k % r])

    for k in range(min(r, n)):
        copy(k).start()
    for k in range(n):
        copy(k).wait()
        pieces[k][1][...] = slots[k % r][...].astype(BF16)
        if k + r < n:
            copy(k + r).start()
        if then is not None and k + 1 == len(early):
            then()


def _time_major(ref, nt):
    return jnp.concatenate([ref[:, t, :] for t in range(nt)], axis=0)


def _row_parts(m, parts):
    return [slice(h * (m // parts), (h + 1) * (m // parts)) for h in range(parts)]


def _conv_gates(hn, win_ref, j, stage, parts=1):
    raw_refs, slot = stage
    raw_ref = raw_refs[j % len(raw_refs)]
    m = hn.shape[0]
    for r in _row_parts(m, parts):
        for sec in range(4):
            raw_ref[slot, sec, r, :] = _dot(hn[r], win_ref[:, _chunk(j, sec)])
    gb, gc, v, z = (raw_ref[slot, sec, 0:m, :] for sec in range(4))
    return gb, gc * v, z


def _conv_prompt_tile(i, tm, x_ref, g_ref, win_ref, cw_ref, cb_ref, wout_ref,
                      x1_ref, nconv_ref, cvh_ref, ybuf_ref, stage, last_tile):
    @pl.when(i == 0)
    def _():
        cvh_ref[...] = jnp.zeros_like(cvh_ref)

    x = x_ref[...]
    hn = _rmsnorm(x, g_ref[0:1, :]).astype(BF16)
    for j in range(N_CHUNKS):
        c = _chunk(j)
        gb, cv, z = _conv_gates(hn, win_ref, j, stage, parts=2 if j == 0 else 1)
        ext = jnp.concatenate([cvh_ref[:, c], cv], axis=0)
        cm1 = pltpu.roll(ext, 1, 0)[CONV_CARRY_ROWS:]
        cm2 = pltpu.roll(ext, 2, 0)[CONV_CARRY_ROWS:]
        conv = cb_ref[:, c] + cm2 * cw_ref[0, 0:1, c]
        conv = conv + cm1 * cw_ref[0, 1:2, c]
        conv = conv + cv * cw_ref[0, 2:3, c]
        ybuf_ref[0:tm, c] = (gb * conv * _silu(z)).astype(BF16)
        cvh_ref[:, c] = cv[tm - CONV_CARRY_ROWS:]
    for r in _row_parts(tm, 2):
        x1_ref[r, :] = x[r] + _dot(ybuf_ref[r, :], wout_ref[...])

    @pl.when(last_tile)
    def _():
        nconv_ref[...] = cvh_ref[CONV_CARRY_ROWS - (CONV_WIDTH - 1):, :]


def _conv_sample_step(nb, nt, xs_ref, sc_ref, g_ref, win_ref, cw_ref, cb_ref, wout_ref,
                      x1s_ref, ncs_ref, ybuf_ref, stage):
    m = nt * nb
    xs = _time_major(xs_ref, nt)
    hn = _rmsnorm(xs, g_ref[0:1, :]).astype(BF16)
    for j in range(N_CHUNKS):
        c = _chunk(j)
        gb, cv, z = _conv_gates(hn, win_ref, j, stage)
        full = jnp.concatenate([sc_ref[:, k, c] for k in range(CONV_WIDTH - 1)] + [cv], axis=0)
        conv = cb_ref[:, c]
        for k in range(CONV_WIDTH):
            conv = conv + full[k * nb:(k + nt) * nb] * cw_ref[0, k:k + 1, c]
        ybuf_ref[0:m, c] = (gb * conv * _silu(z)).astype(BF16)
        for k in range(CONV_WIDTH - 1):
            ncs_ref[:, k, c] = full[(nt + k) * nb:(nt + k + 1) * nb]
    x1 = xs + _dot(ybuf_ref[0:m, :], wout_ref[...])
    for t in range(nt):
        x1s_ref[:, t, :] = x1[t * nb:(t + 1) * nb]


def _conv_layer_kernel(xp_ref, xs_ref, sc_ref, g_ref, win_hbm, cw_ref, cb_ref, wout_hbm, slot_ref,
                       x1p_ref, x1s_ref, ncp_ref, ncs_ref, cvh_ref, ybuf_ref, rawa_ref, rawb_ref,
                       win_ref, wout_ref, wsem_ref,
                       *, tm, tiles_per_seq, n_prompt_steps, nb, nt):
    s = pl.program_id(0)
    raw_refs = (rawa_ref, rawb_ref)
    stage = (raw_refs, slot_ref[0])

    @pl.when(s == 0)
    def _():
        _load_weights_as_bf16([(win_hbm, win_ref), (wout_hbm, wout_ref)], raw_refs, wsem_ref)

    @pl.when(s < n_prompt_steps)
    def _():
        i = s % tiles_per_seq
        _conv_prompt_tile(i, tm, xp_ref, g_ref, win_ref, cw_ref, cb_ref, wout_ref,
                          x1p_ref, ncp_ref, cvh_ref, ybuf_ref, stage, i == tiles_per_seq - 1)

    @pl.when(s >= n_prompt_steps)
    def _():
        _conv_sample_step(nb, nt, xs_ref, sc_ref, g_ref, win_ref, cw_ref, cb_ref, wout_ref,
                          x1s_ref, ncs_ref, ybuf_ref, stage)


def _pool_prompt_tile(i, seq, tm, x1_ref, g_ref, gf_ref, win_ref, wug_ref, ps_ref, wout_ref,
                      y_ref, npool_ref, vh_ref, ybuf_ref, stage, last_tile):
    @pl.when(i == 0)
    def _():
        vh_ref[...] = jnp.zeros_like(vh_ref)

    x1 = x1_ref[...]
    hn1 = _rmsnorm(x1, g_ref[1:2, :]).astype(BF16)
    seen = (i * tm + 1 + lax.broadcasted_iota(jnp.int32, (tm, LANES), 0)).astype(F32)
    raw_refs, slot = stage
    for g, w in enumerate(POOL_WINDOWS):
        c = _chunk(g)
        raw_ref = raw_refs[g % len(raw_refs)]
        for r in _row_parts(tm, 2 if g == 0 else 1):
            raw_ref[slot, 0, r, :] = _dot(hn1[r], wug_ref[:, c])
            raw_ref[slot, 1, r, :] = _dot(hn1[r], win_ref[:, _chunk(g, 1)])
        v = raw_ref[slot, 0]
        s = jnp.concatenate([vh_ref[:, c], v], axis=0)
        shift = 1
        while shift < w:
            s = s + pltpu.roll(s, shift, 0)
            shift *= 2
        inv = 1.0 / jnp.minimum(jnp.float32(w), seen)
        inv = jnp.concatenate([inv] * (CHUNK // LANES), axis=1)
        q = s[POOL_CARRY_ROWS:] * inv - v
        z = raw_ref[slot, 1]
        ybuf_ref[0:tm, c] = (q * ps_ref[:, c] * _silu(z)).astype(BF16)
        vh_ref[:, c] = v[tm - POOL_CARRY_ROWS:]
    kh = EXPAND_WIDTH // 2
    rows = _row_parts(tm, 2)
    lo = [_dot(ybuf_ref[r, 0:kh], wout_ref[0:kh, :]) for r in rows]
    hi = [_dot(ybuf_ref[r, kh:], wout_ref[kh:, :]) for r in rows]
    for r, a, b in zip(rows, lo, hi):
        y_ref[r, :] = _rmsnorm(x1_ref[r, :] + (a + b), gf_ref[...])

    @pl.when(last_tile)
    def _():
        tail = slice(tm - POOL_CARRY_ROWS, tm)
        hn_tail = _rmsnorm(x1_ref[tail, :], g_ref[1:2, :]).astype(BF16)
        u_tail = _dot(hn_tail, win_ref[:, 0:EXPAND_WIDTH])
        for b in range(npool_ref.shape[1]):
            @pl.when(seq == b)
            def _(b=b):
                npool_ref[:, b, :] = u_tail[POOL_CARRY_ROWS - POOL_HIST:, :]


def _pool_sample_step(g, nb, nt, start_pos, x1s_ref, sp_ref, g_ref, gf_ref, win_ref, wgrp_ref, ps_ref,
                      wout_ref, ys_ref, nps_ref, acc_ref, hn_ref, stage):
    w = POOL_WINDOWS[g]
    c = _chunk(g)
    m = nt * nb
    if g == 0:
        x1 = _time_major(x1s_ref, nt)
        acc_ref[...] = x1
        hn_ref[...] = _rmsnorm(x1, g_ref[1:2, :]).astype(BF16)
    hn1 = hn_ref[...]
    raw_refs, slot = stage
    raw_ref = raw_refs[g % len(raw_refs)]
    raw_ref[slot, 0, 0:m, :] = _dot(hn1, win_ref[:, _chunk(g, 0)])
    raw_ref[slot, 1, 0:m, :] = _dot(hn1, win_ref[:, _chunk(g, 1)])
    u = raw_ref[slot, 0, 0:m, :]

    def full(r):
        if r < POOL_HIST:
            return sp_ref[r]
        return u[(r - POOL_HIST) * nb:(r - POOL_HIST + 1) * nb]

    parts = []
    for t in range(nt):
        win = full(POOL_HIST + t)
        for k in range(1, w):
            win = win + full(POOL_HIST + t - k)
        inv = 1.0 / float(min(w, start_pos + t + 1))
        parts.append(win * inv - full(POOL_HIST + t))
    p = jnp.concatenate(parts, axis=0).astype(BF16)
    raw_ref[slot, 2, 0:m, :] = _dot(p, wgrp_ref[_chunk(g), :])
    q, z = raw_ref[slot, 2, 0:m, :], raw_ref[slot, 1, 0:m, :]
    y = (q * ps_ref[:, c] * _silu(z)).astype(BF16)
    acc_ref[...] += _dot(y, wout_ref[c, :])

    for r in range(POOL_HIST):
        nps_ref[r] = full(r + nt)
    if g == N_POOL_GROUPS - 1:
        ys = _rmsnorm(acc_ref[...], gf_ref[...])
        for t in range(nt):
            ys_ref[:, t, :] = ys[t * nb:(t + 1) * nb]


def _pool_layer_kernel(x1p_ref, x1s_ref, sp_ref, g_ref, gf_ref, win_hbm, wgrp_hbm, ps_ref, wout_hbm, slot_ref,
                       yp_ref, ys_ref, npp_ref, nps_ref, vh_ref, ybuf_ref, acc_ref, hn_ref,
                       rawa_ref, rawb_ref, win_ref, wgrp_ref, wout_ref, wug_ref, wsem_ref,
                       *, tm, tiles_per_seq, n_prompt_steps, nb, nt, start_pos):
    s = pl.program_id(0)
    raw_refs = (rawa_ref, rawb_ref)
    stage = (raw_refs, slot_ref[0])
    phase = jnp.maximum(s - n_prompt_steps, 0) % N_POOL_GROUPS

    @pl.when(s == 0)
    def _():
        def fuse_group_weights():
            for g in range(N_POOL_GROUPS):
                c = _chunk(g)
                wug_ref[:, c] = _dot(win_ref[:, c], wgrp_ref[c, :]).astype(BF16)

        u_blocks = EXPAND_WIDTH // rawa_ref.shape[-1]
        _load_weights_as_bf16([(win_hbm, win_ref), (wgrp_hbm, wgrp_ref), (wout_hbm, wout_ref)],
                              raw_refs, wsem_ref,
                              first=lambda w, i, j: w == 1 or (w == 0 and j < u_blocks),
                              then=fuse_group_weights)

    @pl.when(s < n_prompt_steps)
    def _():
        i = s % tiles_per_seq
        _pool_prompt_tile(i, s // tiles_per_seq, tm, x1p_ref, g_ref, gf_ref, win_ref, wug_ref, ps_ref, wout_ref,
                          yp_ref, npp_ref, vh_ref, ybuf_ref, stage, i == tiles_per_seq - 1)

    for g in range(N_POOL_GROUPS):
        @pl.when(jnp.logical_and(s >= n_prompt_steps, phase == g))
        def _(g=g):
            _pool_sample_step(g, nb, nt, start_pos, x1s_ref, sp_ref, g_ref, gf_ref, win_ref, wgrp_ref,
                              ps_ref, wout_ref, ys_ref, nps_ref, acc_ref, hn_ref, stage)


def _resident(shape):
    zeros = (0,) * len(shape)
    return pl.BlockSpec(shape, lambda s: zeros, pipeline_mode=pl.Buffered(1))


def _layer_calls(xp, xs, sc, sp, g, gf, win0, cw, cb, wout0, win1, wgrp, ps, wout1):
    B, S, D = xp.shape
    NB, NT, _ = xs.shape
    E = EXPAND_WIDTH
    tm = PROMPT_TILE
    assert S % tm == 0 and tm >= POOL_CARRY_ROWS and NB % SAMPLE_SPLIT == 0
    tiles = S // tm
    n_prompt = B * tiles
    nb = NB // SAMPLE_SPLIT
    m = NT * nb
    assert nb % SUBLANES == 0 and m <= tm

    def prompt_tile(s):
        sc_ = jnp.minimum(s, n_prompt - 1)
        return (sc_ // tiles, sc_ % tiles, 0)

    def prompt_seq(s):
        return (jnp.minimum(s, n_prompt - 1) // tiles, 0, 0)

    def conv_part(s):
        return (jnp.maximum(s - n_prompt, 0), 0, 0)

    def pool_part(s):
        return (jnp.maximum(s - n_prompt, 0) // N_POOL_GROUPS, 0, 0)

    def pool_group(s):
        q = jnp.maximum(s - n_prompt, 0)
        return (0, q // N_POOL_GROUPS, q % N_POOL_GROUPS)

    tile_bytes = 4 * tm * D * 4
    temp_bytes = 16 * tm * CHUNK * 4 + tm * E * 2
    stage_bytes = 2 * 4 * tm * CHUNK * 4
    headroom = 4 << 20
    stage_slot = jnp.zeros((1,), jnp.int32)
    hbm = pl.BlockSpec(memory_space=pl.ANY)
    conv_planes, pool_planes = 4, 3

    conv_sample_bytes = 2 * 4 * (2 * NT * nb * D + 2 * (CONV_WIDTH - 1) * nb * E)
    conv_limit = min(2 * (win0.size + wout0.size) + tile_bytes + temp_bytes + stage_bytes
                     + conv_sample_bytes + headroom, VMEM_PHYSICAL_BYTES - headroom)
    x1p, x1s, ncp, ncs = pl.pallas_call(
        functools.partial(_conv_layer_kernel, tm=tm, tiles_per_seq=tiles, n_prompt_steps=n_prompt,
                          nb=nb, nt=NT),
        grid=(n_prompt + SAMPLE_SPLIT,),
        in_specs=[
            pl.BlockSpec((None, tm, D), prompt_tile),
            pl.BlockSpec((nb, NT, D), conv_part),
            pl.BlockSpec((nb, CONV_WIDTH - 1, E), conv_part),
            _resident(g.shape), hbm, _resident(cw.shape), _resident(cb.shape), hbm,
            pl.BlockSpec(memory_space=pltpu.SMEM),
        ],
        out_specs=[
            pl.BlockSpec((None, tm, D), prompt_tile),
            pl.BlockSpec((nb, NT, D), conv_part),
            pl.BlockSpec((None, CONV_WIDTH - 1, E), prompt_seq),
            pl.BlockSpec((nb, CONV_WIDTH - 1, E), conv_part),
        ],
        out_shape=[
            jax.ShapeDtypeStruct((B, S, D), F32),
            jax.ShapeDtypeStruct((NB, NT, D), F32),
            jax.ShapeDtypeStruct((B, CONV_WIDTH - 1, E), F32),
            jax.ShapeDtypeStruct((NB, CONV_WIDTH - 1, E), F32),
        ],
        scratch_shapes=[
            pltpu.VMEM((CONV_CARRY_ROWS, E), F32),
            pltpu.VMEM((tm, E), BF16),
            pltpu.VMEM((1, conv_planes, tm, CHUNK), F32),
            pltpu.VMEM((1, conv_planes, tm, CHUNK), F32),
            pltpu.VMEM(win0.shape, BF16),
            pltpu.VMEM(wout0.shape, BF16),
            pltpu.SemaphoreType.DMA((2 * conv_planes,)),
        ],
        compiler_params=pltpu.CompilerParams(
            dimension_semantics=("arbitrary",), vmem_limit_bytes=conv_limit),
        name="conv_layer",
    )(xp, xs, sc, g, win0, cw, cb, wout0, stage_slot)

    pool_sample_bytes = 2 * 4 * (2 * NT * nb * D + 2 * nb * (POOL_HIST + 1) * CHUNK) + m * D * 6
    pool_limit = min(2 * (win1.size + wgrp.size + wout1.size + D * E) + tile_bytes + temp_bytes
                     + stage_bytes + pool_sample_bytes + headroom, VMEM_PHYSICAL_BYTES - headroom)
    yp, ys, npp, nps = pl.pallas_call(
        functools.partial(_pool_layer_kernel, tm=tm, tiles_per_seq=tiles, n_prompt_steps=n_prompt,
                          nb=nb, nt=NT, start_pos=PAST_LEN),
        grid=(n_prompt + SAMPLE_SPLIT * N_POOL_GROUPS,),
        in_specs=[
            pl.BlockSpec((None, tm, D), prompt_tile),
            pl.BlockSpec((nb, NT, D), pool_part),
            pl.BlockSpec((POOL_HIST, nb, CHUNK), pool_group),
            _resident(g.shape), _resident(gf.shape), hbm, hbm, _resident(ps.shape), hbm,
            pl.BlockSpec(memory_space=pltpu.SMEM),
        ],
        out_specs=[
            pl.BlockSpec((None, tm, D), prompt_tile),
            pl.BlockSpec((nb, NT, D), pool_part),
            pl.BlockSpec((POOL_HIST, B, E), lambda s: (0, 0, 0)),
            pl.BlockSpec((POOL_HIST, nb, CHUNK), pool_group),
        ],
        out_shape=[
            jax.ShapeDtypeStruct((B, S, D), F32),
            jax.ShapeDtypeStruct((NB, NT, D), F32),
            jax.ShapeDtypeStruct((POOL_HIST, B, E), F32),
            jax.ShapeDtypeStruct((POOL_HIST, NB, E), F32),
        ],
        scratch_shapes=[
            pltpu.VMEM((POOL_CARRY_ROWS, E), F32),
            pltpu.VMEM((tm, E), BF16),
            pltpu.VMEM((m, D), F32),
            pltpu.VMEM((m, D), BF16),
            pltpu.VMEM((1, pool_planes, tm, CHUNK), F32),
            pltpu.VMEM((1, pool_planes, tm, CHUNK), F32),
            pltpu.VMEM(win1.shape, BF16),
            pltpu.VMEM(wgrp.shape, BF16),
            pltpu.VMEM(wout1.shape, BF16),
            pltpu.VMEM((D, E), BF16),
            pltpu.SemaphoreType.DMA((2 * pool_planes,)),
        ],
        compiler_params=pltpu.CompilerParams(
            dimension_semantics=("arbitrary",), vmem_limit_bytes=pool_limit),
        name="pool_layer",
    )(x1p, x1s, sp, g, gf, win1, wgrp, ps, wout1, stage_slot)
    return yp, ys, ncp, ncs, npp, nps


def kernel(x_prompt, x_sample, state_conv, state_pool, norm_g, final_norm_g,
           conv_w_in, conv_w, conv_b, conv_w_out,
           pool_w_in, pool_w_grp, pool_scale, pool_w_out):
    assert norm_g.shape[0] == 2 and conv_w_in.shape[0] == 1 and pool_w_in.shape[0] == 1
    gf = final_norm_g.reshape(1, D_MODEL)
    win0, wout0, win1, wout1 = conv_w_in[0], conv_w_out[0], pool_w_in[0], pool_w_out[0]
    wgrp = pool_w_grp[0].reshape(EXPAND_WIDTH, POOL_GROUP_WIDTH)
    yp, ys, ncp, ncs, npp, nps = _layer_calls(
        x_prompt, x_sample, state_conv[0], jnp.transpose(state_pool[0], (1, 0, 2)), norm_g, gf,
        win0, conv_w, conv_b, wout0, win1, wgrp, pool_scale, wout1)
    return (yp, ys, ncp[None], ncs[None],
            jnp.transpose(npp, (1, 0, 2))[None], jnp.transpose(nps, (1, 0, 2))[None])
```

```python
import functools

import jax
import jax.numpy as jnp
from jax import lax
from jax.experimental import pallas as pl
from jax.experimental.pallas import tpu as pltpu

D_MODEL = 1024
EXPAND_WIDTH = 2048
CONV_WIDTH = 3
POOL_WINDOWS = (2, 4, 8, 16)
N_POOL_GROUPS = len(POOL_WINDOWS)
POOL_GROUP_WIDTH = EXPAND_WIDTH // N_POOL_GROUPS
POOL_HIST = max(POOL_WINDOWS) - 1
PAST_LEN = 16384
RMS_EPS = 1e-6
LOG2_E = 1.4426950408889634

LANES = 128
SUBLANES = 8
CONV_CARRY_ROWS = SUBLANES
POOL_CARRY_ROWS = 2 * SUBLANES
CHUNK = POOL_GROUP_WIDTH
N_CHUNKS = EXPAND_WIDTH // CHUNK
PROMPT_TILE = 512
SAMPLE_SPLIT = 2
VMEM_PHYSICAL_BYTES = 64 * 1024 * 1024

BF16 = jnp.bfloat16
F32 = jnp.float32


def _dot(a, b):
    return jnp.dot(a, b, preferred_element_type=F32)


def _rmsnorm(x, g):
    r = lax.rsqrt(jnp.mean(x * x, axis=-1, keepdims=True) + RMS_EPS)
    return x * r * g


def _silu(z):
    return z / (1.0 + jnp.exp2(z * (-LOG2_E)))


def _chunk(j, section=0):
    lo = section * EXPAND_WIDTH + j * CHUNK
    return slice(lo, lo + CHUNK)


def _weight_pieces(src_hbm, dst_ref, piece):
    pr, pc = piece
    rows, cols = src_hbm.shape
    assert rows % pr == 0 and cols % pc == 0
    pairs = []
    for i in range(rows // pr):
        for j in range(cols // pc):
            r, c = pl.ds(i * pr, pr), pl.ds(j * pc, pc)
            pairs.append((src_hbm.at[r, c], dst_ref.at[r, c]))
    return pairs


def _load_weights_as_bf16(weights, raw_refs, sem_ref, first=None, then=None):
    slots = [r.at[0, k] for r in raw_refs for k in range(r.shape[1])]
    pr, pc = slots[0].shape
    early, late = [], []
    for w, (src_hbm, dst_ref) in enumerate(weights):
        cols = src_hbm.shape[1] // pc
        for k, piece in enumerate(_weight_pieces(src_hbm, dst_ref, (pr, pc))):
            (early if first is not None and first(w, k // cols, k % cols) else late).append(piece)
    pieces = early + late
    n, r = len(pieces), len(slots)

    def copy(k):
        return pltpu.make_async_copy(pieces[k][0], slots[k % r], sem_ref.at[k % r])

    for k in range(min(r, n)):
        copy(k).start()
    for k in range(n):
        copy(k).wait()
        pieces[k][1][...] = slots[k % r][...].astype(BF16)
        if k + r < n:
            copy(k + r).start()
        if then is not None and k + 1 == len(early):
            then()


def _time_major(ref, nt):
    return jnp.concatenate([ref[:, t, :] for t in range(nt)], axis=0)


def _row_parts(m, parts):
    return [slice(h * (m // parts), (h + 1) * (m // parts)) for h in range(parts)]


def _conv_gates(hn, win_ref, j, stage, parts=1):
    raw_refs, slot = stage
    raw_ref = raw_refs[j % len(raw_refs)]
    m = hn.shape[0]
    for r in _row_parts(m, parts):
        for sec in range(4):
            raw_ref[slot, sec, r, :] = _dot(hn[r], win_ref[:, _chunk(j, sec)])
    gb, gc, v, z = (raw_ref[slot, sec, 0:m, :] for sec in range(4))
    return gb, gc * v, z


def _conv_prompt_tile(i, tm, x_ref, g_ref, win_ref, cw_ref, cb_ref, wout_ref,
                      x1_ref, nconv_ref, cvh_ref, ybuf_ref, stage, last_tile):
    @pl.when(i == 0)
    def _():
        cvh_ref[...] = jnp.zeros_like(cvh_ref)

    x = x_ref[...]
    hn = _rmsnorm(x, g_ref[0:1, :]).astype(BF16)
    for j in range(N_CHUNKS):
        c = _chunk(j)
        gb, cv, z = _conv_gates(hn, win_ref, j, stage, parts=2 if j == 0 else 1)
        ext = jnp.concatenate([cvh_ref[:, c], cv], axis=0)
        cm1 = pltpu.roll(ext, 1, 0)[CONV_CARRY_ROWS:]
        cm2 = pltpu.roll(ext, 2, 0)[CONV_CARRY_ROWS:]
        conv = cb_ref[:, c] + cm2 * cw_ref[0, 0:1, c]
        conv = conv + cm1 * cw_ref[0, 1:2, c]
        conv = conv + cv * cw_ref[0, 2:3, c]
        ybuf_ref[0:tm, c] = (gb * conv * _silu(z)).astype(BF16)
        cvh_ref[:, c] = cv[tm - CONV_CARRY_ROWS:]
    for r in _row_parts(tm, 2):
        x1_ref[r, :] = x[r] + _dot(ybuf_ref[r, :], wout_ref[...])

    @pl.when(last_tile)
    def _():
        nconv_ref[...] = cvh_ref[CONV_CARRY_ROWS - (CONV_WIDTH - 1):, :]


def _conv_sample_step(nb, nt, xs_ref, sc_ref, g_ref, win_ref, cw_ref, cb_ref, wout_ref,
                      x1s_ref, ncs_ref, ybuf_ref, stage):
    m = nt * nb
    xs = _time_major(xs_ref, nt)
    hn = _rmsnorm(xs, g_ref[0:1, :]).astype(BF16)
    for j in range(N_CHUNKS):
        c = _chunk(j)
        gb, cv, z = _conv_gates(hn, win_ref, j, stage)
        full = jnp.concatenate([sc_ref[:, k, c] for k in range(CONV_WIDTH - 1)] + [cv], axis=0)
        conv = cb_ref[:, c]
        for k in range(CONV_WIDTH):
            conv = conv + full[k * nb:(k + nt) * nb] * cw_ref[0, k:k + 1, c]
        ybuf_ref[0:m, c] = (gb * conv * _silu(z)).astype(BF16)
        for k in range(CONV_WIDTH - 1):
            ncs_ref[:, k, c] = full[(nt + k) * nb:(nt + k + 1) * nb]
    x1 = xs + _dot(ybuf_ref[0:m, :], wout_ref[...])
    for t in range(nt):
        x1s_ref[:, t, :] = x1[t * nb:(t + 1) * nb]


def _conv_layer_kernel(xp_ref, xs_ref, sc_ref, g_ref, win_hbm, cw_ref, cb_ref, wout_hbm, slot_ref,
                       x1p_ref, x1s_ref, ncp_ref, ncs_ref, cvh_ref, ybuf_ref, rawa_ref, rawb_ref,
                       win_ref, wout_ref, wsem_ref,
                       *, tm, tiles_per_seq, n_prompt_steps, nb, nt):
    s = pl.program_id(0)
    raw_refs = (rawa_ref, rawb_ref)
    stage = (raw_refs, slot_ref[0])

    @pl.when(s == 0)
    def _():
        _load_weights_as_bf16([(win_hbm, win_ref), (wout_hbm, wout_ref)], raw_refs, wsem_ref)

    @pl.when(s < n_prompt_steps)
    def _():
        i = s % tiles_per_seq
        _conv_prompt_tile(i, tm, xp_ref, g_ref, win_ref, cw_ref, cb_ref, wout_ref,
                          x1p_ref, ncp_ref, cvh_ref, ybuf_ref, stage, i == tiles_per_seq - 1)

    @pl.when(s >= n_prompt_steps)
    def _():
        _conv_sample_step(nb, nt, xs_ref, sc_ref, g_ref, win_ref, cw_ref, cb_ref, wout_ref,
                          x1s_ref, ncs_ref, ybuf_ref, stage)


def _pool_prompt_tile(i, seq, tm, x1_ref, g_ref, gf_ref, win_ref, wug_ref, ps_ref, wout_ref,
                      y_ref, tails_ref, vh_ref, ybuf_ref, stage, last_tile):
    @pl.when(i == 0)
    def _():
        vh_ref[...] = jnp.zeros_like(vh_ref)

    x1 = x1_ref[...]
    hn1 = _rmsnorm(x1, g_ref[1:2, :]).astype(BF16)
    seen = (i * tm + 1 + lax.broadcasted_iota(jnp.int32, (tm, LANES), 0)).astype(F32)
    raw_refs, slot = stage
    for g, w in enumerate(POOL_WINDOWS):
        c = _chunk(g)
        raw_ref = raw_refs[g % len(raw_refs)]
        for r in _row_parts(tm, 2 if g == 0 else 1):
            raw_ref[slot, 0, r, :] = _dot(hn1[r], wug_ref[:, c])
            raw_ref[slot, 1, r, :] = _dot(hn1[r], win_ref[:, _chunk(g, 1)])
        v = raw_ref[slot, 0]
        s = jnp.concatenate([vh_ref[:, c], v], axis=0)
        shift = 1
        while shift < w:
            s = s + pltpu.roll(s, shift, 0)
            shift *= 2
        inv = 1.0 / jnp.minimum(jnp.float32(w), seen)
        inv = jnp.concatenate([inv] * (CHUNK // LANES), axis=1)
        q = s[POOL_CARRY_ROWS:] * inv - v
        z = raw_ref[slot, 1]
        ybuf_ref[0:tm, c] = (q * ps_ref[:, c] * _silu(z)).astype(BF16)
        vh_ref[:, c] = v[tm - POOL_CARRY_ROWS:]
    kh = EXPAND_WIDTH // 2
    rows = _row_parts(tm, 2)
    lo = [_dot(ybuf_ref[r, 0:kh], wout_ref[0:kh, :]) for r in rows]
    hi = [_dot(ybuf_ref[r, kh:], wout_ref[kh:, :]) for r in rows]
    for r, a, b in zip(rows, lo, hi):
        y_ref[r, :] = _rmsnorm(x1_ref[r, :] + (a + b), gf_ref[...])

    @pl.when(last_tile)
    def _():
        tail = slice(tm - POOL_CARRY_ROWS, tm)
        at = pl.multiple_of(seq * POOL_CARRY_ROWS, POOL_CARRY_ROWS)
        tails_ref[pl.ds(at, POOL_CARRY_ROWS), :] = _rmsnorm(x1_ref[tail, :], g_ref[1:2, :]).astype(BF16)


def _pool_prompt_state(tails_ref, win_ref, npool_ref):
    u_tails = _dot(tails_ref[...], win_ref[:, 0:EXPAND_WIDTH])
    for b in range(npool_ref.shape[1]):
        npool_ref[:, b, :] = u_tails[(b + 1) * POOL_CARRY_ROWS - POOL_HIST:(b + 1) * POOL_CARRY_ROWS, :]


def _pool_sample_step(g, nb, nt, start_pos, x1s_ref, sp_ref, g_ref, gf_ref, win_ref, wgrp_ref, ps_ref,
                      wout_ref, ys_ref, nps_ref, acc_ref, hn_ref, stage):
    w = POOL_WINDOWS[g]
    c = _chunk(g)
    m = nt * nb
    if g == 0:
        x1 = _time_major(x1s_ref, nt)
        acc_ref[...] = x1
        hn_ref[...] = _rmsnorm(x1, g_ref[1:2, :]).astype(BF16)
    hn1 = hn_ref[...]
    raw_refs, slot = stage
    raw_ref = raw_refs[g % len(raw_refs)]
    raw_ref[slot, 0, 0:m, :] = _dot(hn1, win_ref[:, _chunk(g, 0)])
    raw_ref[slot, 1, 0:m, :] = _dot(hn1, win_ref[:, _chunk(g, 1)])
    u = raw_ref[slot, 0, 0:m, :]

    def full(r):
        if r < POOL_HIST:
            return sp_ref[r]
        return u[(r - POOL_HIST) * nb:(r - POOL_HIST + 1) * nb]

    parts = []
    for t in range(nt):
        win = full(POOL_HIST + t)
        for k in range(1, w):
            win = win + full(POOL_HIST + t - k)
        inv = 1.0 / float(min(w, start_pos + t + 1))
        parts.append(win * inv - full(POOL_HIST + t))
    p = jnp.concatenate(parts, axis=0).astype(BF16)
    raw_ref[slot, 2, 0:m, :] = _dot(p, wgrp_ref[_chunk(g), :])
    q, z = raw_ref[slot, 2, 0:m, :], raw_ref[slot, 1, 0:m, :]
    y = (q * ps_ref[:, c] * _silu(z)).astype(BF16)
    acc_ref[...] += _dot(y, wout_ref[c, :])

    for r in range(POOL_HIST):
        nps_ref[r] = full(r + nt)
    if g == N_POOL_GROUPS - 1:
        ys = _rmsnorm(acc_ref[...], gf_ref[...])
        for t in range(nt):
            ys_ref[:, t, :] = ys[t * nb:(t + 1) * nb]


def _pool_layer_kernel(x1p_ref, x1s_ref, sp_ref, g_ref, gf_ref, win_hbm, wgrp_hbm, ps_ref, wout_hbm, slot_ref,
                       yp_ref, ys_ref, npp_ref, nps_ref, vh_ref, ybuf_ref, acc_ref, hn_ref,
                       rawa_ref, rawb_ref, win_ref, wgrp_ref, wout_ref, wug_ref, tails_ref, wsem_ref,
                       *, tm, tiles_per_seq, n_prompt_steps, nb, nt, start_pos):
    s = pl.program_id(0)
    raw_refs = (rawa_ref, rawb_ref)
    stage = (raw_refs, slot_ref[0])
    phase = jnp.maximum(s - n_prompt_steps, 0) % N_POOL_GROUPS

    @pl.when(s == 0)
    def _():
        def fuse_group_weights():
            for g in range(N_POOL_GROUPS):
                c = _chunk(g)
                wug_ref[:, c] = _dot(win_ref[:, c], wgrp_ref[c, :]).astype(BF16)

        u_blocks = EXPAND_WIDTH // rawa_ref.shape[-1]
        _load_weights_as_bf16([(win_hbm, win_ref), (wgrp_hbm, wgrp_ref), (wout_hbm, wout_ref)],
                              raw_refs, wsem_ref,
                              first=lambda w, i, j: w == 1 or (w == 0 and j < u_blocks),
                              then=fuse_group_weights)

    @pl.when(s < n_prompt_steps)
    def _():
        i = s % tiles_per_seq
        _pool_prompt_tile(i, s // tiles_per_seq, tm, x1p_ref, g_ref, gf_ref, win_ref, wug_ref, ps_ref, wout_ref,
                          yp_ref, tails_ref, vh_ref, ybuf_ref, stage, i == tiles_per_seq - 1)

    @pl.when(s == n_prompt_steps)
    def _():
        _pool_prompt_state(tails_ref, win_ref, npp_ref)

    for g in range(N_POOL_GROUPS):
        @pl.when(jnp.logical_and(s >= n_prompt_steps, phase == g))
        def _(g=g):
            _pool_sample_step(g, nb, nt, start_pos, x1s_ref, sp_ref, g_ref, gf_ref, win_ref, wgrp_ref,
                              ps_ref, wout_ref, ys_ref, nps_ref, acc_ref, hn_ref, stage)


def _resident(shape):
    zeros = (0,) * len(shape)
    return pl.BlockSpec(shape, lambda s: zeros, pipeline_mode=pl.Buffered(1))


def _layer_calls(xp, xs, sc, sp, g, gf, win0, cw, cb, wout0, win1, wgrp, ps, wout1):
    B, S, D = xp.shape
    NB, NT, _ = xs.shape
    E = EXPAND_WIDTH
    tm = PROMPT_TILE
    assert S % tm == 0 and tm >= POOL_CARRY_ROWS and NB % SAMPLE_SPLIT == 0
    tiles = S // tm
    n_prompt = B * tiles
    nb = NB // SAMPLE_SPLIT
    m = NT * nb
    assert nb % SUBLANES == 0 and m <= tm

    def prompt_tile(s):
        sc_ = jnp.minimum(s, n_prompt - 1)
        return (sc_ // tiles, sc_ % tiles, 0)

    def prompt_seq(s):
        return (jnp.minimum(s, n_prompt - 1) // tiles, 0, 0)

    def conv_part(s):
        return (jnp.maximum(s - n_prompt, 0), 0, 0)

    def pool_part(s):
        return (jnp.maximum(s - n_prompt, 0) // N_POOL_GROUPS, 0, 0)

    def pool_group(s):
        q = jnp.maximum(s - n_prompt, 0)
        return (0, q // N_POOL_GROUPS, q % N_POOL_GROUPS)

    tile_bytes = 4 * tm * D * 4
    temp_bytes = 16 * tm * CHUNK * 4 + tm * E * 2
    stage_bytes = 2 * 4 * tm * CHUNK * 4
    headroom = 4 << 20
    stage_slot = jnp.zeros((1,), jnp.int32)
    hbm = pl.BlockSpec(memory_space=pl.ANY)
    conv_planes, pool_planes = 4, 3

    conv_sample_bytes = 2 * 4 * (2 * NT * nb * D + 2 * (CONV_WIDTH - 1) * nb * E)
    conv_limit = min(2 * (win0.size + wout0.size) + tile_bytes + temp_bytes + stage_bytes
                     + conv_sample_bytes + headroom, VMEM_PHYSICAL_BYTES - headroom)
    x1p, x1s, ncp, ncs = pl.pallas_call(
        functools.partial(_conv_layer_kernel, tm=tm, tiles_per_seq=tiles, n_prompt_steps=n_prompt,
                          nb=nb, nt=NT),
        grid=(n_prompt + SAMPLE_SPLIT,),
        in_specs=[
            pl.BlockSpec((None, tm, D), prompt_tile),
            pl.BlockSpec((nb, NT, D), conv_part),
            pl.BlockSpec((nb, CONV_WIDTH - 1, E), conv_part),
            _resident(g.shape), hbm, _resident(cw.shape), _resident(cb.shape), hbm,
            pl.BlockSpec(memory_space=pltpu.SMEM),
        ],
        out_specs=[
            pl.BlockSpec((None, tm, D), prompt_tile),
            pl.BlockSpec((nb, NT, D), conv_part),
            pl.BlockSpec((None, CONV_WIDTH - 1, E), prompt_seq),
            pl.BlockSpec((nb, CONV_WIDTH - 1, E), conv_part),
        ],
        out_shape=[
            jax.ShapeDtypeStruct((B, S, D), F32),
            jax.ShapeDtypeStruct((NB, NT, D), F32),
            jax.ShapeDtypeStruct((B, CONV_WIDTH - 1, E), F32),
            jax.ShapeDtypeStruct((NB, CONV_WIDTH - 1, E), F32),
        ],
        scratch_shapes=[
            pltpu.VMEM((CONV_CARRY_ROWS, E), F32),
            pltpu.VMEM((tm, E), BF16),
            pltpu.VMEM((1, conv_planes, tm, CHUNK), F32),
            pltpu.VMEM((1, conv_planes, tm, CHUNK), F32),
            pltpu.VMEM(win0.shape, BF16),
            pltpu.VMEM(wout0.shape, BF16),
            pltpu.SemaphoreType.DMA((2 * conv_planes,)),
        ],
        compiler_params=pltpu.CompilerParams(
            dimension_semantics=("arbitrary",), vmem_limit_bytes=conv_limit),
        name="conv_layer",
    )(xp, xs, sc, g, win0, cw, cb, wout0, stage_slot)

    pool_sample_bytes = 2 * 4 * (2 * NT * nb * D + 2 * nb * (POOL_HIST + 1) * CHUNK) + m * D * 6
    pool_limit = min(2 * (win1.size + wgrp.size + wout1.size + D * E) + tile_bytes + temp_bytes
                     + stage_bytes + pool_sample_bytes + headroom, VMEM_PHYSICAL_BYTES - headroom)
    yp, ys, npp, nps = pl.pallas_call(
        functools.partial(_pool_layer_kernel, tm=tm, tiles_per_seq=tiles, n_prompt_steps=n_prompt,
                          nb=nb, nt=NT, start_pos=PAST_LEN),
        grid=(n_prompt + SAMPLE_SPLIT * N_POOL_GROUPS,),
        in_specs=[
            pl.BlockSpec((None, tm, D), prompt_tile),
            pl.BlockSpec((nb, NT, D), pool_part),
            pl.BlockSpec((POOL_HIST, nb, CHUNK), pool_group),
            _resident(g.shape), _resident(gf.shape), hbm, hbm, _resident(ps.shape), hbm,
            pl.BlockSpec(memory_space=pltpu.SMEM),
        ],
        out_specs=[
            pl.BlockSpec((None, tm, D), prompt_tile),
            pl.BlockSpec((nb, NT, D), pool_part),
            pl.BlockSpec((POOL_HIST, B, E), lambda s: (0, 0, 0)),
            pl.BlockSpec((POOL_HIST, nb, CHUNK), pool_group),
        ],
        out_shape=[
            jax.ShapeDtypeStruct((B, S, D), F32),
            jax.ShapeDtypeStruct((NB, NT, D), F32),
            jax.ShapeDtypeStruct((POOL_HIST, B, E), F32),
            jax.ShapeDtypeStruct((POOL_HIST, NB, E), F32),
        ],
        scratch_shapes=[
            pltpu.VMEM((POOL_CARRY_ROWS, E), F32),
            pltpu.VMEM((tm, E), BF16),
            pltpu.VMEM((m, D), F32),
            pltpu.VMEM((m, D), BF16),
            pltpu.VMEM((1, pool_planes, tm, CHUNK), F32),
            pltpu.VMEM((1, pool_planes, tm, CHUNK), F32),
            pltpu.VMEM(win1.shape, BF16),
            pltpu.VMEM(wgrp.shape, BF16),
            pltpu.VMEM(wout1.shape, BF16),
            pltpu.VMEM((D, E), BF16),
            pltpu.VMEM((B * POOL_CARRY_ROWS, D), BF16),
            pltpu.SemaphoreType.DMA((2 * pool_planes,)),
        ],
        compiler_params=pltpu.CompilerParams(
            dimension_semantics=("arbitrary",), vmem_limit_bytes=pool_limit),
        name="pool_layer",
    )(x1p, x1s, sp, g, gf, win1, wgrp, ps, wout1, stage_slot)
    return yp, ys, ncp, ncs, npp, nps


def kernel(x_prompt, x_sample, state_conv, state_pool, norm_g, final_norm_g,
           conv_w_in, conv_w, conv_b, conv_w_out,
           pool_w_in, pool_w_grp, pool_scale, pool_w_out):
    assert norm_g.shape[0] == 2 and conv_w_in.shape[0] == 1 and pool_w_in.shape[0] == 1
    gf = final_norm_g.reshape(1, D_MODEL)
    win0, wout0, win1, wout1 = conv_w_in[0], conv_w_out[0], pool_w_in[0], pool_w_out[0]
    wgrp = pool_w_grp[0].reshape(EXPAND_WIDTH, POOL_GROUP_WIDTH)
    yp, ys, ncp, ncs, npp, nps = _layer_calls(
        x_prompt, x_sample, state_conv[0], jnp.transpose(state_pool[0], (1, 0, 2)), norm_g, gf,
        win0, conv_w, conv_b, wout0, win1, wgrp, pool_scale, wout1)
    return (yp, ys, ncp[None], ncs[None],
            jnp.transpose(npp, (1, 0, 2))[None], jnp.transpose(nps, (1, 0, 2))[None])
```

```python
import functools

import jax
import jax.numpy as jnp
from jax import lax
from jax.experimental import pallas as pl
from jax.experimental.pallas import tpu as pltpu

D_MODEL = 1024
EXPAND_WIDTH = 2048
CONV_WIDTH = 3
POOL_WINDOWS = (2, 4, 8, 16)
N_POOL_GROUPS = len(POOL_WINDOWS)
POOL_GROUP_WIDTH = EXPAND_WIDTH // N_POOL_GROUPS
POOL_HIST = max(POOL_WINDOWS) - 1
PAST_LEN = 16384
RMS_EPS = 1e-6
LOG2_E = 1.4426950408889634

LANES = 128
SUBLANES = 8
CONV_CARRY_ROWS = SUBLANES
POOL_CARRY_ROWS = 2 * SUBLANES
CHUNK = POOL_GROUP_WIDTH
N_CHUNKS = EXPAND_WIDTH // CHUNK
CONV_CHUNK = 256
N_CONV_CHUNKS = EXPAND_WIDTH // CONV_CHUNK
PROMPT_TILE = 512
SAMPLE_SPLIT = 2
VMEM_PHYSICAL_BYTES = 64 * 1024 * 1024

BF16 = jnp.bfloat16
F32 = jnp.float32


def _dot(a, b):
    return jnp.dot(a, b, preferred_element_type=F32)


def _rmsnorm(x, g):
    r = lax.rsqrt(jnp.mean(x * x, axis=-1, keepdims=True) + RMS_EPS)
    return x * r * g


def _silu(z):
    return z / (1.0 + jnp.exp2(z * (-LOG2_E)))


def _chunk(j, section=0, width=CHUNK):
    lo = section * EXPAND_WIDTH + j * width
    return slice(lo, lo + width)


def _weight_pieces(src_hbm, dst_ref, piece):
    pr, pc = piece
    rows, cols = src_hbm.shape
    assert rows % pr == 0 and cols % pc == 0
    pairs = []
    for i in range(rows // pr):
        for j in range(cols // pc):
            r, c = pl.ds(i * pr, pr), pl.ds(j * pc, pc)
            pairs.append((src_hbm.at[r, c], dst_ref.at[r, c]))
    return pairs


def _load_weights_as_bf16(weights, raw_refs, sem_ref, first=None, then=None):
    slots = [r.at[0, k] for r in raw_refs for k in range(r.shape[1])]
    pr, pc = slots[0].shape
    early, late = [], []
    for w, (src_hbm, dst_ref) in enumerate(weights):
        cols = src_hbm.shape[1] // pc
        for k, piece in enumerate(_weight_pieces(src_hbm, dst_ref, (pr, pc))):
            (early if first is not None and first(w, k // cols, k % cols) else late).append(piece)
    pieces = early + late
    n, r = len(pieces), len(slots)

    def copy(k):
        return pltpu.make_async_copy(pieces[k][0], slots[k % r], sem_ref.at[k % r])

    for k in range(min(r, n)):
        copy(k).start()
    for k in range(n):
        copy(k).wait()
        pieces[k][1][...] = slots[k % r][...].astype(BF16)
        if k + r < n:
            copy(k + r).start()
        if then is not None and k + 1 == len(early):
            then()


def _time_major(ref, nt):
    return jnp.concatenate([ref[:, t, :] for t in range(nt)], axis=0)


def _row_parts(m, parts):
    return [slice(h * (m // parts), (h + 1) * (m // parts)) for h in range(parts)]


def _conv_gates(hn, win_ref, j, stage, parts=1):
    raw_refs, slot = stage
    raw_ref = raw_refs[j % len(raw_refs)]
    m = hn.shape[0]
    for r in _row_parts(m, parts):
        for sec in range(4):
            raw_ref[slot, sec, r, :] = _dot(hn[r], win_ref[:, _chunk(j, sec, CONV_CHUNK)])
    gb, gc, v, z = (raw_ref[slot, sec, 0:m, :] for sec in range(4))
    return gb, gc * v, z


def _conv_prompt_tile(i, tm, x_ref, g_ref, win_ref, cw_ref, cb_ref, wout_ref,
                      x1_ref, nconv_ref, cvh_ref, ybuf_ref, stage, last_tile):
    @pl.when(i == 0)
    def _():
        cvh_ref[...] = jnp.zeros_like(cvh_ref)

    x = x_ref[...]
    hn = _rmsnorm(x, g_ref[0:1, :]).astype(BF16)
    for j in range(N_CONV_CHUNKS):
        c = _chunk(j, width=CONV_CHUNK)
        gb, cv, z = _conv_gates(hn, win_ref, j, stage, parts=2 if j == 0 else 1)
        ext = jnp.concatenate([cvh_ref[:, c], cv], axis=0)
        cm1 = pltpu.roll(ext, 1, 0)[CONV_CARRY_ROWS:]
        cm2 = pltpu.roll(ext, 2, 0)[CONV_CARRY_ROWS:]
        conv = cb_ref[:, c] + cm2 * cw_ref[0, 0:1, c]
        conv = conv + cm1 * cw_ref[0, 1:2, c]
        conv = conv + cv * cw_ref[0, 2:3, c]
        ybuf_ref[0:tm, c] = (gb * conv * _silu(z)).astype(BF16)
        cvh_ref[:, c] = cv[tm - CONV_CARRY_ROWS:]
    for r in _row_parts(tm, 2):
        x1_ref[r, :] = x[r] + _dot(ybuf_ref[r, :], wout_ref[...])

    @pl.when(last_tile)
    def _():
        nconv_ref[...] = cvh_ref[CONV_CARRY_ROWS - (CONV_WIDTH - 1):, :]


def _conv_sample_step(nb, nt, xs_ref, sc_ref, g_ref, win_ref, cw_ref, cb_ref, wout_ref,
                      x1s_ref, ncs_ref, ybuf_ref, stage):
    m = nt * nb
    xs = _time_major(xs_ref, nt)
    hn = _rmsnorm(xs, g_ref[0:1, :]).astype(BF16)
    for j in range(N_CONV_CHUNKS):
        c = _chunk(j, width=CONV_CHUNK)
        gb, cv, z = _conv_gates(hn, win_ref, j, stage)
        full = jnp.concatenate([sc_ref[:, k, c] for k in range(CONV_WIDTH - 1)] + [cv], axis=0)
        conv = cb_ref[:, c]
        for k in range(CONV_WIDTH):
            conv = conv + full[k * nb:(k + nt) * nb] * cw_ref[0, k:k + 1, c]
        ybuf_ref[0:m, c] = (gb * conv * _silu(z)).astype(BF16)
        for k in range(CONV_WIDTH - 1):
            ncs_ref[:, k, c] = full[(nt + k) * nb:(nt + k + 1) * nb]
    x1 = xs + _dot(ybuf_ref[0:m, :], wout_ref[...])
    for t in range(nt):
        x1s_ref[:, t, :] = x1[t * nb:(t + 1) * nb]


def _conv_layer_kernel(xp_ref, xs_ref, sc_ref, g_ref, win_hbm, cw_ref, cb_ref, wout_hbm, slot_ref,
                       x1p_ref, x1s_ref, ncp_ref, ncs_ref, cvh_ref, ybuf_ref, rawa_ref, rawb_ref,
                       win_ref, wout_ref, wsem_ref,
                       *, tm, tiles_per_seq, n_prompt_steps, nb, nt):
    s = pl.program_id(0)
    raw_refs = (rawa_ref, rawb_ref)
    stage = (raw_refs, slot_ref[0])

    @pl.when(s == 0)
    def _():
        _load_weights_as_bf16([(win_hbm, win_ref), (wout_hbm, wout_ref)], raw_refs, wsem_ref)

    @pl.when(s < n_prompt_steps)
    def _():
        i = s % tiles_per_seq
        _conv_prompt_tile(i, tm, xp_ref, g_ref, win_ref, cw_ref, cb_ref, wout_ref,
                          x1p_ref, ncp_ref, cvh_ref, ybuf_ref, stage, i == tiles_per_seq - 1)

    @pl.when(s >= n_prompt_steps)
    def _():
        _conv_sample_step(nb, nt, xs_ref, sc_ref, g_ref, win_ref, cw_ref, cb_ref, wout_ref,
                          x1s_ref, ncs_ref, ybuf_ref, stage)


def _pool_prompt_tile(i, seq, tm, x1_ref, g_ref, gf_ref, win_ref, wug_ref, ps_ref, wout_ref,
                      y_ref, tails_ref, vh_ref, ybuf_ref, stage, last_tile):
    @pl.when(i == 0)
    def _():
        vh_ref[...] = jnp.zeros_like(vh_ref)

    x1 = x1_ref[...]
    hn1 = _rmsnorm(x1, g_ref[1:2, :]).astype(BF16)
    seen = (i * tm + 1 + lax.broadcasted_iota(jnp.int32, (tm, LANES), 0)).astype(F32)
    raw_refs, slot = stage
    for g, w in enumerate(POOL_WINDOWS):
        c = _chunk(g)
        raw_ref = raw_refs[g % len(raw_refs)]
        for r in _row_parts(tm, 2 if g == 0 else 1):
            raw_ref[slot, 0, r, :] = _dot(hn1[r], wug_ref[:, c])
            raw_ref[slot, 1, r, :] = _dot(hn1[r], win_ref[:, _chunk(g, 1)])
        v = raw_ref[slot, 0]
        s = jnp.concatenate([vh_ref[:, c], v], axis=0)
        shift = 1
        while shift < w:
            s = s + pltpu.roll(s, shift, 0)
            shift *= 2
        inv = 1.0 / jnp.minimum(jnp.float32(w), seen)
        inv = jnp.concatenate([inv] * (CHUNK // LANES), axis=1)
        q = s[POOL_CARRY_ROWS:] * inv - v
        z = raw_ref[slot, 1]
        ybuf_ref[0:tm, c] = (q * ps_ref[:, c] * _silu(z)).astype(BF16)
        vh_ref[:, c] = v[tm - POOL_CARRY_ROWS:]
    kh = EXPAND_WIDTH // 2
    rows = _row_parts(tm, 2)
    lo = [_dot(ybuf_ref[r, 0:kh], wout_ref[0:kh, :]) for r in rows]
    hi = [_dot(ybuf_ref[r, kh:], wout_ref[kh:, :]) for r in rows]
    for r, a, b in zip(rows, lo, hi):
        y_ref[r, :] = _rmsnorm(x1_ref[r, :] + (a + b), gf_ref[...])

    @pl.when(last_tile)
    def _():
        tail = slice(tm - POOL_CARRY_ROWS, tm)
        at = pl.multiple_of(seq * POOL_CARRY_ROWS, POOL_CARRY_ROWS)
        tails_ref[pl.ds(at, POOL_CARRY_ROWS), :] = _rmsnorm(x1_ref[tail, :], g_ref[1:2, :]).astype(BF16)


def _pool_prompt_state(tails_ref, win_ref, npool_ref):
    u_tails = _dot(tails_ref[...], win_ref[:, 0:EXPAND_WIDTH])
    for b in range(npool_ref.shape[1]):
        npool_ref[:, b, :] = u_tails[(b + 1) * POOL_CARRY_ROWS - POOL_HIST:(b + 1) * POOL_CARRY_ROWS, :]


def _pool_sample_step(g, nb, nt, start_pos, x1s_ref, sp_ref, g_ref, gf_ref, win_ref, wgrp_ref, ps_ref,
                      wout_ref, ys_ref, nps_ref, acc_ref, hn_ref, stage):
    w = POOL_WINDOWS[g]
    c = _chunk(g)
    m = nt * nb
    if g == 0:
        x1 = _time_major(x1s_ref, nt)
        acc_ref[...] = x1
        hn_ref[...] = _rmsnorm(x1, g_ref[1:2, :]).astype(BF16)
    hn1 = hn_ref[...]
    raw_refs, slot = stage
    raw_ref = raw_refs[g % len(raw_refs)]
    raw_ref[slot, 0, 0:m, :] = _dot(hn1, win_ref[:, _chunk(g, 0)])
    raw_ref[slot, 1, 0:m, :] = _dot(hn1, win_ref[:, _chunk(g, 1)])
    u = raw_ref[slot, 0, 0:m, :]

    def full(r):
        if r < POOL_HIST:
            return sp_ref[r]
        return u[(r - POOL_HIST) * nb:(r - POOL_HIST + 1) * nb]

    parts = []
    for t in range(nt):
        win = full(POOL_HIST + t)
        for k in range(1, w):
            win = win + full(POOL_HIST + t - k)
        inv = 1.0 / float(min(w, start_pos + t + 1))
        parts.append(win * inv - full(POOL_HIST + t))
    p = jnp.concatenate(parts, axis=0).astype(BF16)
    raw_ref[slot, 2, 0:m, :] = _dot(p, wgrp_ref[_chunk(g), :])
    q, z = raw_ref[slot, 2, 0:m, :], raw_ref[slot, 1, 0:m, :]
    y = (q * ps_ref[:, c] * _silu(z)).astype(BF16)
    acc_ref[...] += _dot(y, wout_ref[c, :])

    for r in range(POOL_HIST):
        nps_ref[r] = full(r + nt)
    if g == N_POOL_GROUPS - 1:
        ys = _rmsnorm(acc_ref[...], gf_ref[...])
        for t in range(nt):
            ys_ref[:, t, :] = ys[t * nb:(t + 1) * nb]


def _pool_layer_kernel(x1p_ref, x1s_ref, sp_ref, g_ref, gf_ref, win_hbm, wgrp_hbm, ps_ref, wout_hbm, slot_ref,
                       yp_ref, ys_ref, npp_ref, nps_ref, vh_ref, ybuf_ref, acc_ref, hn_ref,
                       rawa_ref, rawb_ref, win_ref, wgrp_ref, wout_ref, wug_ref, tails_ref, wsem_ref,
                       *, tm, tiles_per_seq, n_prompt_steps, nb, nt, start_pos):
    s = pl.program_id(0)
    raw_refs = (rawa_ref, rawb_ref)
    stage = (raw_refs, slot_ref[0])
    phase = jnp.maximum(s - n_prompt_steps, 0) % N_POOL_GROUPS

    @pl.when(s == 0)
    def _():
        def fuse_group_weights():
            for g in range(N_POOL_GROUPS):
                c = _chunk(g)
                wug_ref[:, c] = _dot(win_ref[:, c], wgrp_ref[c, :]).astype(BF16)

        u_blocks = EXPAND_WIDTH // rawa_ref.shape[-1]
        _load_weights_as_bf16([(win_hbm, win_ref), (wgrp_hbm, wgrp_ref), (wout_hbm, wout_ref)],
                              raw_refs, wsem_ref,
                              first=lambda w, i, j: w == 1 or (w == 0 and j < u_blocks),
                              then=fuse_group_weights)

    @pl.when(s < n_prompt_steps)
    def _():
        i = s % tiles_per_seq
        _pool_prompt_tile(i, s // tiles_per_seq, tm, x1p_ref, g_ref, gf_ref, win_ref, wug_ref, ps_ref, wout_ref,
                          yp_ref, tails_ref, vh_ref, ybuf_ref, stage, i == tiles_per_seq - 1)

    @pl.when(s == n_prompt_steps)
    def _():
        _pool_prompt_state(tails_ref, win_ref, npp_ref)

    for g in range(N_POOL_GROUPS):
        @pl.when(jnp.logical_and(s >= n_prompt_steps, phase == g))
        def _(g=g):
            _pool_sample_step(g, nb, nt, start_pos, x1s_ref, sp_ref, g_ref, gf_ref, win_ref, wgrp_ref,
                              ps_ref, wout_ref, ys_ref, nps_ref, acc_ref, hn_ref, stage)


def _resident(shape):
    zeros = (0,) * len(shape)
    return pl.BlockSpec(shape, lambda s: zeros, pipeline_mode=pl.Buffered(1))


def _layer_calls(xp, xs, sc, sp, g, gf, win0, cw, cb, wout0, win1, wgrp, ps, wout1):
    B, S, D = xp.shape
    NB, NT, _ = xs.shape
    E = EXPAND_WIDTH
    tm = PROMPT_TILE
    assert S % tm == 0 and tm >= POOL_CARRY_ROWS and NB % SAMPLE_SPLIT == 0
    tiles = S // tm
    n_prompt = B * tiles
    nb = NB // SAMPLE_SPLIT
    m = NT * nb
    assert nb % SUBLANES == 0 and m <= tm

    def prompt_tile(s):
        sc_ = jnp.minimum(s, n_prompt - 1)
        return (sc_ // tiles, sc_ % tiles, 0)

    def prompt_seq(s):
        return (jnp.minimum(s, n_prompt - 1) // tiles, 0, 0)

    def conv_part(s):
        return (jnp.maximum(s - n_prompt, 0), 0, 0)

    def pool_part(s):
        return (jnp.maximum(s - n_prompt, 0) // N_POOL_GROUPS, 0, 0)

    def pool_group(s):
        q = jnp.maximum(s - n_prompt, 0)
        return (0, q // N_POOL_GROUPS, q % N_POOL_GROUPS)

    tile_bytes = 4 * tm * D * 4
    temp_bytes = 16 * tm * CHUNK * 4 + tm * E * 2
    stage_bytes = 2 * 4 * tm * CHUNK * 4
    headroom = 4 << 20
    stage_slot = jnp.zeros((1,), jnp.int32)
    hbm = pl.BlockSpec(memory_space=pl.ANY)
    conv_planes, pool_planes = 4, 3

    conv_sample_bytes = 2 * 4 * (2 * NT * nb * D + 2 * (CONV_WIDTH - 1) * nb * E)
    conv_limit = min(2 * (win0.size + wout0.size) + tile_bytes + temp_bytes + stage_bytes
                     + conv_sample_bytes + headroom, VMEM_PHYSICAL_BYTES - headroom)
    x1p, x1s, ncp, ncs = pl.pallas_call(
        functools.partial(_conv_layer_kernel, tm=tm, tiles_per_seq=tiles, n_prompt_steps=n_prompt,
                          nb=nb, nt=NT),
        grid=(n_prompt + SAMPLE_SPLIT,),
        in_specs=[
            pl.BlockSpec((None, tm, D), prompt_tile),
            pl.BlockSpec((nb, NT, D), conv_part),
            pl.BlockSpec((nb, CONV_WIDTH - 1, E), conv_part),
            _resident(g.shape), hbm, _resident(cw.shape), _resident(cb.shape), hbm,
            pl.BlockSpec(memory_space=pltpu.SMEM),
        ],
        out_specs=[
            pl.BlockSpec((None, tm, D), prompt_tile),
            pl.BlockSpec((nb, NT, D), conv_part),
            pl.BlockSpec((None, CONV_WIDTH - 1, E), prompt_seq),
            pl.BlockSpec((nb, CONV_WIDTH - 1, E), conv_part),
        ],
        out_shape=[
            jax.ShapeDtypeStruct((B, S, D), F32),
            jax.ShapeDtypeStruct((NB, NT, D), F32),
            jax.ShapeDtypeStruct((B, CONV_WIDTH - 1, E), F32),
            jax.ShapeDtypeStruct((NB, CONV_WIDTH - 1, E), F32),
        ],
        scratch_shapes=[
            pltpu.VMEM((CONV_CARRY_ROWS, E), F32),
            pltpu.VMEM((tm, E), BF16),
            pltpu.VMEM((1, conv_planes, tm, CONV_CHUNK), F32),
            pltpu.VMEM((1, conv_planes, tm, CONV_CHUNK), F32),
            pltpu.VMEM(win0.shape, BF16),
            pltpu.VMEM(wout0.shape, BF16),
            pltpu.SemaphoreType.DMA((2 * conv_planes,)),
        ],
        compiler_params=pltpu.CompilerParams(
            dimension_semantics=("arbitrary",), vmem_limit_bytes=conv_limit),
        name="conv_layer",
    )(xp, xs, sc, g, win0, cw, cb, wout0, stage_slot)

    pool_sample_bytes = 2 * 4 * (2 * NT * nb * D + 2 * nb * (POOL_HIST + 1) * CHUNK) + m * D * 6
    pool_limit = min(2 * (win1.size + wgrp.size + wout1.size + D * E) + tile_bytes + temp_bytes
                     + stage_bytes + pool_sample_bytes + headroom, VMEM_PHYSICAL_BYTES - headroom)
    yp, ys, npp, nps = pl.pallas_call(
        functools.partial(_pool_layer_kernel, tm=tm, tiles_per_seq=tiles, n_prompt_steps=n_prompt,
                          nb=nb, nt=NT, start_pos=PAST_LEN),
        grid=(n_prompt + SAMPLE_SPLIT * N_POOL_GROUPS,),
        in_specs=[
            pl.BlockSpec((None, tm, D), prompt_tile),
            pl.BlockSpec((nb, NT, D), pool_part),
            pl.BlockSpec((POOL_HIST, nb, CHUNK), pool_group),
            _resident(g.shape), _resident(gf.shape), hbm, hbm, _resident(ps.shape), hbm,
            pl.BlockSpec(memory_space=pltpu.SMEM),
        ],
        out_specs=[
            pl.BlockSpec((None, tm, D), prompt_tile),
            pl.BlockSpec((nb, NT, D), pool_part),
            pl.BlockSpec((POOL_HIST, B, E), lambda s: (0, 0, 0)),
            pl.BlockSpec((POOL_HIST, nb, CHUNK), pool_group),
        ],
        out_shape=[
            jax.ShapeDtypeStruct((B, S, D), F32),
            jax.ShapeDtypeStruct((NB, NT, D), F32),
            jax.ShapeDtypeStruct((POOL_HIST, B, E), F32),
            jax.ShapeDtypeStruct((POOL_HIST, NB, E), F32),
        ],
        scratch_shapes=[
            pltpu.VMEM((POOL_CARRY_ROWS, E), F32),
            pltpu.VMEM((tm, E), BF16),
            pltpu.VMEM((m, D), F32),
            pltpu.VMEM((m, D), BF16),
            pltpu.VMEM((1, pool_planes, tm, CHUNK), F32),
            pltpu.VMEM((1, pool_planes, tm, CHUNK), F32),
            pltpu.VMEM(win1.shape, BF16),
            pltpu.VMEM(wgrp.shape, BF16),
            pltpu.VMEM(wout1.shape, BF16),
            pltpu.VMEM((D, E), BF16),
            pltpu.VMEM((B * POOL_CARRY_ROWS, D), BF16),
            pltpu.SemaphoreType.DMA((2 * pool_planes,)),
        ],
        compiler_params=pltpu.CompilerParams(
            dimension_semantics=("arbitrary",), vmem_limit_bytes=pool_limit),
        name="pool_layer",
    )(x1p, x1s, sp, g, gf, win1, wgrp, ps, wout1, stage_slot)
    return yp, ys, ncp, ncs, npp, nps


def kernel(x_prompt, x_sample, state_conv, state_pool, norm_g, final_norm_g,
           conv_w_in, conv_w, conv_b, conv_w_out,
           pool_w_in, pool_w_grp, pool_scale, pool_w_out):
    assert norm_g.shape[0] == 2 and conv_w_in.shape[0] == 1 and pool_w_in.shape[0] == 1
    gf = final_norm_g.reshape(1, D_MODEL)
    win0, wout0, win1, wout1 = conv_w_in[0], conv_w_out[0], pool_w_in[0], pool_w_out[0]
    wgrp = pool_w_grp[0].reshape(EXPAND_WIDTH, POOL_GROUP_WIDTH)
    yp, ys, ncp, ncs, npp, nps = _layer_calls(
        x_prompt, x_sample, state_conv[0], jnp.transpose(state_pool[0], (1, 0, 2)), norm_g, gf,
        win0, conv_w, conv_b, wout0, win1, wgrp, pool_scale, wout1)
    return (yp, ys, ncp[None], ncs[None],
            jnp.transpose(npp, (1, 0, 2))[None], jnp.transpose(nps, (1, 0, 2))[None])
```

```python
import functools

import jax
import jax.numpy as jnp
from jax import lax
from jax.experimental import pallas as pl
from jax.experimental.pallas import tpu as pltpu

D_MODEL = 1024
EXPAND_WIDTH = 2048
CONV_WIDTH = 3
POOL_WINDOWS = (2, 4, 8, 16)
N_POOL_GROUPS = len(POOL_WINDOWS)
POOL_GROUP_WIDTH = EXPAND_WIDTH // N_POOL_GROUPS
POOL_HIST = max(POOL_WINDOWS) - 1
PAST_LEN = 16384
RMS_EPS = 1e-6
LOG2_E = 1.4426950408889634

LANES = 128
SUBLANES = 8
CONV_CARRY_ROWS = SUBLANES
POOL_CARRY_ROWS = 2 * SUBLANES
CHUNK = POOL_GROUP_WIDTH
N_CHUNKS = EXPAND_WIDTH // CHUNK
PROMPT_TILE = 512
SAMPLE_SPLIT = 2
VMEM_PHYSICAL_BYTES = 64 * 1024 * 1024

BF16 = jnp.bfloat16
F32 = jnp.float32


def _dot(a, b):
    return jnp.dot(a, b, preferred_element_type=F32)


def _rmsnorm(x, g):
    r = lax.rsqrt(jnp.mean(x * x, axis=-1, keepdims=True) + RMS_EPS)
    return x * r * g


def _silu(z):
    return z / (1.0 + jnp.exp2(z * (-LOG2_E)))


def _chunk(j, section=0):
    lo = section * EXPAND_WIDTH + j * CHUNK
    return slice(lo, lo + CHUNK)


def _weight_pieces(src_hbm, dst_ref, piece):
    pr, pc = piece
    rows, cols = src_hbm.shape
    assert rows % pr == 0 and cols % pc == 0
    pairs = []
    for i in range(rows // pr):
        for j in range(cols // pc):
            r, c = pl.ds(i * pr, pr), pl.ds(j * pc, pc)
            pairs.append((src_hbm.at[r, c], dst_ref.at[r, c]))
    return pairs


def _load_weights_as_bf16(weights, raw_refs, sem_ref):
    slots = [r.at[0, k] for r in raw_refs for k in range(r.shape[1])]
    pieces = []
    for src_hbm, dst_ref in weights:
        pieces += _weight_pieces(src_hbm, dst_ref, slots[0].shape)
    n, r = len(pieces), len(slots)

    def copy(k):
        return pltpu.make_async_copy(pieces[k][0], slots[k % r], sem_ref.at[k % r])

    for k in range(min(r, n)):
        copy(k).start()
    for k in range(n):
        copy(k).wait()
        pieces[k][1][...] = slots[k % r][...].astype(BF16)
        if k + r < n:
            copy(k + r).start()


def _time_major(ref, nt):
    return jnp.concatenate([ref[:, t, :] for t in range(nt)], axis=0)


def _row_parts(m, parts):
    return [slice(h * (m // parts), (h + 1) * (m // parts)) for h in range(parts)]


def _conv_gates(hn, win_ref, j, stage, parts=1):
    raw_refs, slot = stage
    raw_ref = raw_refs[j % len(raw_refs)]
    m = hn.shape[0]
    for r in _row_parts(m, parts):
        for sec in range(4):
            raw_ref[slot, sec, r, :] = _dot(hn[r], win_ref[:, _chunk(j, sec)])
    gb, gc, v, z = (raw_ref[slot, sec, 0:m, :] for sec in range(4))
    return gb, gc * v, z


def _conv_prompt_tile(i, tm, x_ref, g_ref, win_ref, cw_ref, cb_ref, wout_ref,
                      x1_ref, nconv_ref, cvh_ref, ybuf_ref, stage, last_tile):
    @pl.when(i == 0)
    def _():
        cvh_ref[...] = jnp.zeros_like(cvh_ref)

    x = x_ref[...]
    hn = _rmsnorm(x, g_ref[0:1, :]).astype(BF16)
    for j in range(N_CHUNKS):
        c = _chunk(j)
        gb, cv, z = _conv_gates(hn, win_ref, j, stage, parts=2 if j == 0 else 1)
        ext = jnp.concatenate([cvh_ref[:, c], cv], axis=0)
        cm1 = pltpu.roll(ext, 1, 0)[CONV_CARRY_ROWS:]
        cm2 = pltpu.roll(ext, 2, 0)[CONV_CARRY_ROWS:]
        conv = cb_ref[:, c] + cm2 * cw_ref[0, 0:1, c]
        conv = conv + cm1 * cw_ref[0, 1:2, c]
        conv = conv + cv * cw_ref[0, 2:3, c]
        ybuf_ref[0:tm, c] = (gb * conv * _silu(z)).astype(BF16)
        cvh_ref[:, c] = cv[tm - CONV_CARRY_ROWS:]
    for r in _row_parts(tm, 2):
        x1_ref[r, :] = x[r] + _dot(ybuf_ref[r, :], wout_ref[...])

    @pl.when(last_tile)
    def _():
        nconv_ref[...] = cvh_ref[CONV_CARRY_ROWS - (CONV_WIDTH - 1):, :]


def _conv_sample_step(nb, nt, xs_ref, sc_ref, g_ref, win_ref, cw_ref, cb_ref, wout_ref,
                      x1s_ref, ncs_ref, ybuf_ref, stage):
    m = nt * nb
    xs = _time_major(xs_ref, nt)
    hn = _rmsnorm(xs, g_ref[0:1, :]).astype(BF16)
    for j in range(N_CHUNKS):
        c = _chunk(j)
        gb, cv, z = _conv_gates(hn, win_ref, j, stage)
        full = jnp.concatenate([sc_ref[:, k, c] for k in range(CONV_WIDTH - 1)] + [cv], axis=0)
        conv = cb_ref[:, c]
        for k in range(CONV_WIDTH):
            conv = conv + full[k * nb:(k + nt) * nb] * cw_ref[0, k:k + 1, c]
        ybuf_ref[0:m, c] = (gb * conv * _silu(z)).astype(BF16)
        for k in range(CONV_WIDTH - 1):
            ncs_ref[:, k, c] = full[(nt + k) * nb:(nt + k + 1) * nb]
    x1 = xs + _dot(ybuf_ref[0:m, :], wout_ref[...])
    for t in range(nt):
        x1s_ref[:, t, :] = x1[t * nb:(t + 1) * nb]


def _conv_layer_kernel(xp_ref, xs_ref, sc_ref, g_ref, win_hbm, cw_ref, cb_ref, wout_hbm, slot_ref,
                       nxt_in_ref, nxt_grp_ref, nxt_out_ref,
                       x1p_ref, x1s_ref, ncp_ref, ncs_ref, nxt_in_bf_ref, nxt_grp_bf_ref, nxt_out_bf_ref,
                       cvh_ref, ybuf_ref, rawa_ref, rawb_ref, win_ref, wout_ref, wsem_ref,
                       *, tm, tiles_per_seq, n_prompt_steps, nb, nt):
    s = pl.program_id(0)
    raw_refs = (rawa_ref, rawb_ref)
    stage = (raw_refs, slot_ref[0])

    @pl.when(s == 0)
    def _():
        _load_weights_as_bf16([(win_hbm, win_ref), (wout_hbm, wout_ref)], raw_refs, wsem_ref)

    @pl.when(s < n_prompt_steps)
    def _():
        i = s % tiles_per_seq
        _conv_prompt_tile(i, tm, xp_ref, g_ref, win_ref, cw_ref, cb_ref, wout_ref,
                          x1p_ref, ncp_ref, cvh_ref, ybuf_ref, stage, i == tiles_per_seq - 1)
        for src_ref, dst_ref in ((nxt_in_ref, nxt_in_bf_ref), (nxt_grp_ref, nxt_grp_bf_ref),
                                 (nxt_out_ref, nxt_out_bf_ref)):
            dst_ref[...] = src_ref[...].astype(BF16)

    @pl.when(s >= n_prompt_steps)
    def _():
        _conv_sample_step(nb, nt, xs_ref, sc_ref, g_ref, win_ref, cw_ref, cb_ref, wout_ref,
                          x1s_ref, ncs_ref, ybuf_ref, stage)


def _pool_prompt_tile(i, seq, tm, x1_ref, g_ref, gf_ref, win_ref, wug_ref, ps_ref, wout_ref,
                      y_ref, tails_ref, vh_ref, ybuf_ref, stage, last_tile):
    @pl.when(i == 0)
    def _():
        vh_ref[...] = jnp.zeros_like(vh_ref)

    x1 = x1_ref[...]
    hn1 = _rmsnorm(x1, g_ref[1:2, :]).astype(BF16)
    seen = (i * tm + 1 + lax.broadcasted_iota(jnp.int32, (tm, LANES), 0)).astype(F32)
    raw_refs, slot = stage
    for g, w in enumerate(POOL_WINDOWS):
        c = _chunk(g)
        raw_ref = raw_refs[g % len(raw_refs)]
        for r in _row_parts(tm, 2 if g == 0 else 1):
            raw_ref[slot, 0, r, :] = _dot(hn1[r], wug_ref[:, c])
            raw_ref[slot, 1, r, :] = _dot(hn1[r], win_ref[:, _chunk(g, 1)])
        v = raw_ref[slot, 0]
        s = jnp.concatenate([vh_ref[:, c], v], axis=0)
        shift = 1
        while shift < w:
            s = s + pltpu.roll(s, shift, 0)
            shift *= 2
        inv = 1.0 / jnp.minimum(jnp.float32(w), seen)
        inv = jnp.concatenate([inv] * (CHUNK // LANES), axis=1)
        q = s[POOL_CARRY_ROWS:] * inv - v
        z = raw_ref[slot, 1]
        ybuf_ref[0:tm, c] = (q * ps_ref[:, c] * _silu(z)).astype(BF16)
        vh_ref[:, c] = v[tm - POOL_CARRY_ROWS:]
    kh = EXPAND_WIDTH // 2
    rows = _row_parts(tm, 2)
    lo = [_dot(ybuf_ref[r, 0:kh], wout_ref[0:kh, :]) for r in rows]
    hi = [_dot(ybuf_ref[r, kh:], wout_ref[kh:, :]) for r in rows]
    for r, a, b in zip(rows, lo, hi):
        y_ref[r, :] = _rmsnorm(x1_ref[r, :] + (a + b), gf_ref[...])

    @pl.when(last_tile)
    def _():
        tail = slice(tm - POOL_CARRY_ROWS, tm)
        at = pl.multiple_of(seq * POOL_CARRY_ROWS, POOL_CARRY_ROWS)
        tails_ref[pl.ds(at, POOL_CARRY_ROWS), :] = _rmsnorm(x1_ref[tail, :], g_ref[1:2, :]).astype(BF16)


def _pool_prompt_state(tails_ref, win_ref, npool_ref):
    u_tails = _dot(tails_ref[...], win_ref[:, 0:EXPAND_WIDTH])
    for b in range(npool_ref.shape[1]):
        npool_ref[:, b, :] = u_tails[(b + 1) * POOL_CARRY_ROWS - POOL_HIST:(b + 1) * POOL_CARRY_ROWS, :]


def _pool_sample_step(g, nb, nt, start_pos, x1s_ref, sp_ref, g_ref, gf_ref, win_ref, wgrp_ref, ps_ref,
                      wout_ref, ys_ref, nps_ref, acc_ref, hn_ref, stage):
    w = POOL_WINDOWS[g]
    c = _chunk(g)
    m = nt * nb
    if g == 0:
        x1 = _time_major(x1s_ref, nt)
        acc_ref[...] = x1
        hn_ref[...] = _rmsnorm(x1, g_ref[1:2, :]).astype(BF16)
    hn1 = hn_ref[...]
    raw_refs, slot = stage
    raw_ref = raw_refs[g % len(raw_refs)]
    raw_ref[slot, 0, 0:m, :] = _dot(hn1, win_ref[:, _chunk(g, 0)])
    raw_ref[slot, 1, 0:m, :] = _dot(hn1, win_ref[:, _chunk(g, 1)])
    u = raw_ref[slot, 0, 0:m, :]

    def full(r):
        if r < POOL_HIST:
            return sp_ref[r]
        return u[(r - POOL_HIST) * nb:(r - POOL_HIST + 1) * nb]

    parts = []
    for t in range(nt):
        win = full(POOL_HIST + t)
        for k in range(1, w):
            win = win + full(POOL_HIST + t - k)
        inv = 1.0 / float(min(w, start_pos + t + 1))
        parts.append(win * inv - full(POOL_HIST + t))
    p = jnp.concatenate(parts, axis=0).astype(BF16)
    raw_ref[slot, 2, 0:m, :] = _dot(p, wgrp_ref[_chunk(g), :])
    q, z = raw_ref[slot, 2, 0:m, :], raw_ref[slot, 1, 0:m, :]
    y = (q * ps_ref[:, c] * _silu(z)).astype(BF16)
    acc_ref[...] += _dot(y, wout_ref[c, :])

    for r in range(POOL_HIST):
        nps_ref[r] = full(r + nt)
    if g == N_POOL_GROUPS - 1:
        ys = _rmsnorm(acc_ref[...], gf_ref[...])
        for t in range(nt):
            ys_ref[:, t, :] = ys[t * nb:(t + 1) * nb]


def _pool_layer_kernel(x1p_ref, x1s_ref, sp_ref, g_ref, gf_ref, win_hbm, wgrp_hbm, ps_ref, wout_hbm, slot_ref,
                       yp_ref, ys_ref, npp_ref, nps_ref, vh_ref, ybuf_ref, acc_ref, hn_ref,
                       rawa_ref, rawb_ref, win_ref, wgrp_ref, wout_ref, wug_ref, tails_ref, wsem_ref,
                       *, tm, tiles_per_seq, n_prompt_steps, nb, nt, start_pos):
    s = pl.program_id(0)
    raw_refs = (rawa_ref, rawb_ref)
    stage = (raw_refs, slot_ref[0])
    phase = jnp.maximum(s - n_prompt_steps, 0) % N_POOL_GROUPS

    @pl.when(s == 0)
    def _():
        def fuse_group_weights():
            for g in range(N_POOL_GROUPS):
                c = _chunk(g)
                wug_ref[:, c] = _dot(win_ref[:, c], wgrp_ref[c, :]).astype(BF16)

        copies = [pltpu.make_async_copy(src, dst, wsem_ref.at[k]) for k, (src, dst) in
                  enumerate(((win_hbm, win_ref), (wgrp_hbm, wgrp_ref), (wout_hbm, wout_ref)))]
        for copy in copies:
            copy.start()
        copies[0].wait()
        copies[1].wait()
        fuse_group_weights()
        copies[2].wait()

    @pl.when(s < n_prompt_steps)
    def _():
        i = s % tiles_per_seq
        _pool_prompt_tile(i, s // tiles_per_seq, tm, x1p_ref, g_ref, gf_ref, win_ref, wug_ref, ps_ref, wout_ref,
                          yp_ref, tails_ref, vh_ref, ybuf_ref, stage, i == tiles_per_seq - 1)

    @pl.when(s == n_prompt_steps)
    def _():
        _pool_prompt_state(tails_ref, win_ref, npp_ref)

    for g in range(N_POOL_GROUPS):
        @pl.when(jnp.logical_and(s >= n_prompt_steps, phase == g))
        def _(g=g):
            _pool_sample_step(g, nb, nt, start_pos, x1s_ref, sp_ref, g_ref, gf_ref, win_ref, wgrp_ref,
                              ps_ref, wout_ref, ys_ref, nps_ref, acc_ref, hn_ref, stage)


def _resident(shape):
    zeros = (0,) * len(shape)
    return pl.BlockSpec(shape, lambda s: zeros, pipeline_mode=pl.Buffered(1))


def _layer_calls(xp, xs, sc, sp, g, gf, win0, cw, cb, wout0, win1, wgrp, ps, wout1):
    B, S, D = xp.shape
    NB, NT, _ = xs.shape
    E = EXPAND_WIDTH
    tm = PROMPT_TILE
    assert S % tm == 0 and tm >= POOL_CARRY_ROWS and NB % SAMPLE_SPLIT == 0
    tiles = S // tm
    n_prompt = B * tiles
    nb = NB // SAMPLE_SPLIT
    m = NT * nb
    assert nb % SUBLANES == 0 and m <= tm

    def prompt_tile(s):
        sc_ = jnp.minimum(s, n_prompt - 1)
        return (sc_ // tiles, sc_ % tiles, 0)

    def prompt_seq(s):
        return (jnp.minimum(s, n_prompt - 1) // tiles, 0, 0)

    def conv_part(s):
        return (jnp.maximum(s - n_prompt, 0), 0, 0)

    def pool_part(s):
        return (jnp.maximum(s - n_prompt, 0) // N_POOL_GROUPS, 0, 0)

    def pool_group(s):
        q = jnp.maximum(s - n_prompt, 0)
        return (0, q // N_POOL_GROUPS, q % N_POOL_GROUPS)

    tile_bytes = 4 * tm * D * 4
    temp_bytes = 16 * tm * CHUNK * 4 + tm * E * 2
    stage_bytes = 2 * 4 * tm * CHUNK * 4
    headroom = 4 << 20
    stage_slot = jnp.zeros((1,), jnp.int32)
    hbm = pl.BlockSpec(memory_space=pl.ANY)
    conv_planes, pool_planes = 4, 3

    conv_sample_bytes = 2 * 4 * (2 * NT * nb * D + 2 * (CONV_WIDTH - 1) * nb * E)
    next_bytes = 2 * (4 + 2) * (win1.size + wgrp.size + wout1.size) // n_prompt
    conv_limit = min(2 * (win0.size + wout0.size) + tile_bytes + temp_bytes + stage_bytes + next_bytes
                     + conv_sample_bytes + headroom, VMEM_PHYSICAL_BYTES - headroom)
    next_weights = (win1, wgrp, wout1)
    assert all(w.shape[0] % (n_prompt * 2 * SUBLANES) == 0 for w in next_weights)

    def slab(w):
        return pl.BlockSpec((w.shape[0] // n_prompt, w.shape[1]), lambda s: (jnp.minimum(s, n_prompt - 1), 0))

    x1p, x1s, ncp, ncs, win1, wgrp, wout1 = pl.pallas_call(
        functools.partial(_conv_layer_kernel, tm=tm, tiles_per_seq=tiles, n_prompt_steps=n_prompt,
                          nb=nb, nt=NT),
        grid=(n_prompt + SAMPLE_SPLIT,),
        in_specs=[
            pl.BlockSpec((None, tm, D), prompt_tile),
            pl.BlockSpec((nb, NT, D), conv_part),
            pl.BlockSpec((nb, CONV_WIDTH - 1, E), conv_part),
            _resident(g.shape), hbm, _resident(cw.shape), _resident(cb.shape), hbm,
            pl.BlockSpec(memory_space=pltpu.SMEM),
        ] + [slab(w) for w in next_weights],
        out_specs=[
            pl.BlockSpec((None, tm, D), prompt_tile),
            pl.BlockSpec((nb, NT, D), conv_part),
            pl.BlockSpec((None, CONV_WIDTH - 1, E), prompt_seq),
            pl.BlockSpec((nb, CONV_WIDTH - 1, E), conv_part),
        ] + [slab(w) for w in next_weights],
        out_shape=[
            jax.ShapeDtypeStruct((B, S, D), F32),
            jax.ShapeDtypeStruct((NB, NT, D), F32),
            jax.ShapeDtypeStruct((B, CONV_WIDTH - 1, E), F32),
            jax.ShapeDtypeStruct((NB, CONV_WIDTH - 1, E), F32),
        ] + [jax.ShapeDtypeStruct(w.shape, BF16) for w in next_weights],
        scratch_shapes=[
            pltpu.VMEM((CONV_CARRY_ROWS, E), F32),
            pltpu.VMEM((tm, E), BF16),
            pltpu.VMEM((1, conv_planes, tm, CHUNK), F32),
            pltpu.VMEM((1, conv_planes, tm, CHUNK), F32),
            pltpu.VMEM(win0.shape, BF16),
            pltpu.VMEM(wout0.shape, BF16),
            pltpu.SemaphoreType.DMA((2 * conv_planes,)),
        ],
        compiler_params=pltpu.CompilerParams(
            dimension_semantics=("arbitrary",), vmem_limit_bytes=conv_limit),
        name="conv_layer",
    )(xp, xs, sc, g, win0, cw, cb, wout0, stage_slot, *next_weights)

    pool_sample_bytes = 2 * 4 * (2 * NT * nb * D + 2 * nb * (POOL_HIST + 1) * CHUNK) + m * D * 6
    pool_limit = min(2 * (win1.size + wgrp.size + wout1.size + D * E) + tile_bytes + temp_bytes
                     + stage_bytes + pool_sample_bytes + headroom, VMEM_PHYSICAL_BYTES - headroom)
    yp, ys, npp, nps = pl.pallas_call(
        functools.partial(_pool_layer_kernel, tm=tm, tiles_per_seq=tiles, n_prompt_steps=n_prompt,
                          nb=nb, nt=NT, start_pos=PAST_LEN),
        grid=(n_prompt + SAMPLE_SPLIT * N_POOL_GROUPS,),
        in_specs=[
            pl.BlockSpec((None, tm, D), prompt_tile),
            pl.BlockSpec((nb, NT, D), pool_part),
            pl.BlockSpec((POOL_HIST, nb, CHUNK), pool_group),
            _resident(g.shape), _resident(gf.shape), hbm, hbm, _resident(ps.shape), hbm,
            pl.BlockSpec(memory_space=pltpu.SMEM),
        ],
        out_specs=[
            pl.BlockSpec((None, tm, D), prompt_tile),
            pl.BlockSpec((nb, NT, D), pool_part),
            pl.BlockSpec((POOL_HIST, B, E), lambda s: (0, 0, 0)),
            pl.BlockSpec((POOL_HIST, nb, CHUNK), pool_group),
        ],
        out_shape=[
            jax.ShapeDtypeStruct((B, S, D), F32),
            jax.ShapeDtypeStruct((NB, NT, D), F32),
            jax.ShapeDtypeStruct((POOL_HIST, B, E), F32),
            jax.ShapeDtypeStruct((POOL_HIST, NB, E), F32),
        ],
        scratch_shapes=[
            pltpu.VMEM((POOL_CARRY_ROWS, E), F32),
            pltpu.VMEM((tm, E), BF16),
            pltpu.VMEM((m, D), F32),
            pltpu.VMEM((m, D), BF16),
            pltpu.VMEM((1, pool_planes, tm, CHUNK), F32),
            pltpu.VMEM((1, pool_planes, tm, CHUNK), F32),
            pltpu.VMEM(win1.shape, BF16),
            pltpu.VMEM(wgrp.shape, BF16),
            pltpu.VMEM(wout1.shape, BF16),
            pltpu.VMEM((D, E), BF16),
            pltpu.VMEM((B * POOL_CARRY_ROWS, D), BF16),
            pltpu.SemaphoreType.DMA((2 * pool_planes,)),
        ],
        compiler_params=pltpu.CompilerParams(
            dimension_semantics=("arbitrary",), vmem_limit_bytes=pool_limit),
        name="pool_layer",
    )(x1p, x1s, sp, g, gf, win1, wgrp, ps, wout1, stage_slot)
    return yp, ys, ncp, ncs, npp, nps


def kernel(x_prompt, x_sample, state_conv, state_pool, norm_g, final_norm_g,
           conv_w_in, conv_w, conv_b, conv_w_out,
           pool_w_in, pool_w_grp, pool_scale, pool_w_out):
    assert norm_g.shape[0] == 2 and conv_w_in.shape[0] == 1 and pool_w_in.shape[0] == 1
    gf = final_norm_g.reshape(1, D_MODEL)
    win0, wout0, win1, wout1 = conv_w_in[0], conv_w_out[0], pool_w_in[0], pool_w_out[0]
    wgrp = pool_w_grp[0].reshape(EXPAND_WIDTH, POOL_GROUP_WIDTH)
    yp, ys, ncp, ncs, npp, nps = _layer_calls(
        x_prompt, x_sample, state_conv[0], jnp.transpose(state_pool[0], (1, 0, 2)), norm_g, gf,
        win0, conv_w, conv_b, wout0, win1, wgrp, pool_scale, wout1)
    return (yp, ys, ncp[None], ncs[None],
            jnp.transpose(npp, (1, 0, 2))[None], jnp.transpose(nps, (1, 0, 2))[None])
```

```python
import functools

import jax
import jax.numpy as jnp
from jax import lax
from jax.experimental import pallas as pl
from jax.experimental.pallas import tpu as pltpu

D_MODEL = 1024
EXPAND_WIDTH = 2048
CONV_WIDTH = 3
POOL_WINDOWS = (2, 4, 8, 16)
N_POOL_GROUPS = len(POOL_WINDOWS)
POOL_GROUP_WIDTH = EXPAND_WIDTH // N_POOL_GROUPS
POOL_HIST = max(POOL_WINDOWS) - 1
PAST_LEN = 16384
RMS_EPS = 1e-6
LOG2_E = 1.4426950408889634

LANES = 128
SUBLANES = 8
CONV_CARRY_ROWS = SUBLANES
POOL_CARRY_ROWS = 2 * SUBLANES
CHUNK = POOL_GROUP_WIDTH
N_CHUNKS = EXPAND_WIDTH // CHUNK
PROMPT_TILE = 512
SAMPLE_SPLIT = 2
VMEM_PHYSICAL_BYTES = 64 * 1024 * 1024

BF16 = jnp.bfloat16
F32 = jnp.float32


def _dot(a, b):
    return jnp.dot(a, b, preferred_element_type=F32)


def _rmsnorm(x, g):
    r = lax.rsqrt(jnp.mean(x * x, axis=-1, keepdims=True) + RMS_EPS)
    return x * r * g


def _silu(z):
    return z / (1.0 + jnp.exp2(z * (-LOG2_E)))


def _chunk(j, section=0):
    lo = section * EXPAND_WIDTH + j * CHUNK
    return slice(lo, lo + CHUNK)


def _weight_pieces(src_hbm, dst_ref, piece):
    pr, pc = piece
    rows, cols = src_hbm.shape
    assert rows % pr == 0 and cols % pc == 0
    pairs = []
    for i in range(rows // pr):
        for j in range(cols // pc):
            r, c = pl.ds(i * pr, pr), pl.ds(j * pc, pc)
            pairs.append((src_hbm.at[r, c], dst_ref.at[r, c]))
    return pairs


def _load_weights_as_bf16(weights, raw_refs, sem_ref):
    slots = [r.at[0, k] for r in raw_refs for k in range(r.shape[1])]
    pieces = []
    for src_hbm, dst_ref in weights:
        pieces += _weight_pieces(src_hbm, dst_ref, slots[0].shape)
    n, r = len(pieces), len(slots)

    def copy(k):
        return pltpu.make_async_copy(pieces[k][0], slots[k % r], sem_ref.at[k % r])

    for k in range(min(r, n)):
        copy(k).start()
    for k in range(n):
        copy(k).wait()
        pieces[k][1][...] = slots[k % r][...].astype(BF16)
        if k + r < n:
            copy(k + r).start()


def _time_major(ref, nt):
    return jnp.concatenate([ref[:, t, :] for t in range(nt)], axis=0)


def _row_parts(m, parts):
    return [slice(h * (m // parts), (h + 1) * (m // parts)) for h in range(parts)]


def _conv_gates(hn, win_ref, j, stage, parts=1):
    raw_refs, slot = stage
    raw_ref = raw_refs[j % len(raw_refs)]
    m = hn.shape[0]
    for r in _row_parts(m, parts):
        for sec in range(4):
            raw_ref[slot, sec, r, :] = _dot(hn[r], win_ref[:, _chunk(j, sec)])
    gb, gc, v, z = (raw_ref[slot, sec, 0:m, :] for sec in range(4))
    return gb, gc * v, z


def _conv_prompt_tile(i, tm, x_ref, g_ref, win_ref, cw_ref, cb_ref, wout_ref,
                      x1_ref, nconv_ref, cvh_ref, ybuf_ref, stage, last_tile):
    @pl.when(i == 0)
    def _():
        cvh_ref[...] = jnp.zeros_like(cvh_ref)

    x = x_ref[...]
    hn = _rmsnorm(x, g_ref[0:1, :]).astype(BF16)
    for j in range(N_CHUNKS):
        c = _chunk(j)
        gb, cv, z = _conv_gates(hn, win_ref, j, stage, parts=2 if j == 0 else 1)
        ext = jnp.concatenate([cvh_ref[:, c], cv], axis=0)
        cm1 = pltpu.roll(ext, 1, 0)[CONV_CARRY_ROWS:]
        cm2 = pltpu.roll(ext, 2, 0)[CONV_CARRY_ROWS:]
        conv = cb_ref[:, c] + cm2 * cw_ref[:, _chunk(j, 0)]
        conv = conv + cm1 * cw_ref[:, _chunk(j, 1)]
        conv = conv + cv * cw_ref[:, _chunk(j, 2)]
        ybuf_ref[0:tm, c] = (gb * conv * _silu(z)).astype(BF16)
        cvh_ref[:, c] = cv[tm - CONV_CARRY_ROWS:]
    for r in _row_parts(tm, 2):
        x1_ref[r, :] = x[r] + _dot(ybuf_ref[r, :], wout_ref[...])

    @pl.when(last_tile)
    def _():
        nconv_ref[...] = cvh_ref[CONV_CARRY_ROWS - (CONV_WIDTH - 1):, :]


def _conv_sample_step(nb, nt, xs_ref, sc_ref, g_ref, win_ref, cw_ref, cb_ref, wout_ref,
                      x1s_ref, ncs_ref, ybuf_ref, stage):
    m = nt * nb
    xs = _time_major(xs_ref, nt)
    hn = _rmsnorm(xs, g_ref[0:1, :]).astype(BF16)
    for j in range(N_CHUNKS):
        c = _chunk(j)
        gb, cv, z = _conv_gates(hn, win_ref, j, stage)
        full = jnp.concatenate([sc_ref[:, k, c] for k in range(CONV_WIDTH - 1)] + [cv], axis=0)
        conv = cb_ref[:, c]
        for k in range(CONV_WIDTH):
            conv = conv + full[k * nb:(k + nt) * nb] * cw_ref[:, _chunk(j, k)]
        ybuf_ref[0:m, c] = (gb * conv * _silu(z)).astype(BF16)
        for k in range(CONV_WIDTH - 1):
            ncs_ref[:, k, c] = full[(nt + k) * nb:(nt + k + 1) * nb]
    x1 = xs + _dot(ybuf_ref[0:m, :], wout_ref[...])
    for t in range(nt):
        x1s_ref[:, t, :] = x1[t * nb:(t + 1) * nb]


def _conv_layer_kernel(xp_ref, xs_ref, sc_ref, g_ref, win_hbm, cw_ref, cb_ref, wout_hbm, slot_ref,
                       nxt_in_ref, nxt_grp_ref, nxt_out_ref,
                       x1p_ref, x1s_ref, ncp_ref, ncs_ref, nxt_in_bf_ref, nxt_grp_bf_ref, nxt_out_bf_ref,
                       cvh_ref, ybuf_ref, rawa_ref, rawb_ref, win_ref, wout_ref, wsem_ref,
                       *, tm, tiles_per_seq, n_prompt_steps, nb, nt):
    s = pl.program_id(0)
    raw_refs = (rawa_ref, rawb_ref)
    stage = (raw_refs, slot_ref[0])

    @pl.when(s == 0)
    def _():
        _load_weights_as_bf16([(win_hbm, win_ref), (wout_hbm, wout_ref)], raw_refs, wsem_ref)

    @pl.when(s < n_prompt_steps)
    def _():
        i = s % tiles_per_seq
        _conv_prompt_tile(i, tm, xp_ref, g_ref, win_ref, cw_ref, cb_ref, wout_ref,
                          x1p_ref, ncp_ref, cvh_ref, ybuf_ref, stage, i == tiles_per_seq - 1)
        for src_ref, dst_ref in ((nxt_in_ref, nxt_in_bf_ref), (nxt_grp_ref, nxt_grp_bf_ref),
                                 (nxt_out_ref, nxt_out_bf_ref)):
            dst_ref[...] = src_ref[...].astype(BF16)

    @pl.when(s >= n_prompt_steps)
    def _():
        _conv_sample_step(nb, nt, xs_ref, sc_ref, g_ref, win_ref, cw_ref, cb_ref, wout_ref,
                          x1s_ref, ncs_ref, ybuf_ref, stage)


def _pool_prompt_tile(i, seq, tm, x1_ref, g_ref, gf_ref, win_ref, wug_ref, ps_ref, wout_ref,
                      y_ref, tails_ref, vh_ref, ybuf_ref, stage, last_tile):
    @pl.when(i == 0)
    def _():
        vh_ref[...] = jnp.zeros_like(vh_ref)

    x1 = x1_ref[...]
    hn1 = _rmsnorm(x1, g_ref[1:2, :]).astype(BF16)
    seen = (i * tm + 1 + lax.broadcasted_iota(jnp.int32, (tm, LANES), 0)).astype(F32)
    raw_refs, slot = stage
    for g, w in enumerate(POOL_WINDOWS):
        c = _chunk(g)
        raw_ref = raw_refs[g % len(raw_refs)]
        for r in _row_parts(tm, 2 if g == 0 else 1):
            raw_ref[slot, 0, r, :] = _dot(hn1[r], wug_ref[:, c])
            raw_ref[slot, 1, r, :] = _dot(hn1[r], win_ref[:, _chunk(g, 1)])
        v = raw_ref[slot, 0]
        s = jnp.concatenate([vh_ref[:, c], v], axis=0)
        shift = 1
        while shift < w:
            s = s + pltpu.roll(s, shift, 0)
            shift *= 2
        inv = 1.0 / jnp.minimum(jnp.float32(w), seen)
        inv = jnp.concatenate([inv] * (CHUNK // LANES), axis=1)
        q = s[POOL_CARRY_ROWS:] * inv - v
        z = raw_ref[slot, 1]
        ybuf_ref[0:tm, c] = (q * ps_ref[:, c] * _silu(z)).astype(BF16)
        vh_ref[:, c] = v[tm - POOL_CARRY_ROWS:]
    kh = EXPAND_WIDTH // 2
    rows = _row_parts(tm, 2)
    lo = [_dot(ybuf_ref[r, 0:kh], wout_ref[0:kh, :]) for r in rows]
    hi = [_dot(ybuf_ref[r, kh:], wout_ref[kh:, :]) for r in rows]
    for r, a, b in zip(rows, lo, hi):
        y_ref[r, :] = _rmsnorm(x1_ref[r, :] + (a + b), gf_ref[...])

    @pl.when(last_tile)
    def _():
        tail = slice(tm - POOL_CARRY_ROWS, tm)
        at = pl.multiple_of(seq * POOL_CARRY_ROWS, POOL_CARRY_ROWS)
        tails_ref[pl.ds(at, POOL_CARRY_ROWS), :] = _rmsnorm(x1_ref[tail, :], g_ref[1:2, :]).astype(BF16)


def _pool_prompt_state(tails_ref, win_ref, npool_ref):
    u_tails = _dot(tails_ref[...], win_ref[:, 0:EXPAND_WIDTH])
    for b in range(npool_ref.shape[1]):
        npool_ref[:, b, :] = u_tails[(b + 1) * POOL_CARRY_ROWS - POOL_HIST:(b + 1) * POOL_CARRY_ROWS, :]


def _pool_sample_step(g, nb, nt, start_pos, x1s_ref, sp_ref, g_ref, gf_ref, win_ref, wgrp_ref, ps_ref,
                      wout_ref, ys_ref, nps_ref, acc_ref, hn_ref, stage):
    w = POOL_WINDOWS[g]
    c = _chunk(g)
    m = nt * nb
    if g == 0:
        x1 = _time_major(x1s_ref, nt)
        acc_ref[...] = x1
        hn_ref[...] = _rmsnorm(x1, g_ref[1:2, :]).astype(BF16)
    hn1 = hn_ref[...]
    raw_refs, slot = stage
    raw_ref = raw_refs[g % len(raw_refs)]
    raw_ref[slot, 0, 0:m, :] = _dot(hn1, win_ref[:, _chunk(g, 0)])
    raw_ref[slot, 1, 0:m, :] = _dot(hn1, win_ref[:, _chunk(g, 1)])
    u = raw_ref[slot, 0, 0:m, :]

    def full(r):
        if r < POOL_HIST:
            return sp_ref[r]
        return u[(r - POOL_HIST) * nb:(r - POOL_HIST + 1) * nb]

    parts = []
    for t in range(nt):
        win = full(POOL_HIST + t)
        for k in range(1, w):
            win = win + full(POOL_HIST + t - k)
        inv = 1.0 / float(min(w, start_pos + t + 1))
        parts.append(win * inv - full(POOL_HIST + t))
    p = jnp.concatenate(parts, axis=0).astype(BF16)
    raw_ref[slot, 2, 0:m, :] = _dot(p, wgrp_ref[_chunk(g), :])
    q, z = raw_ref[slot, 2, 0:m, :], raw_ref[slot, 1, 0:m, :]
    y = (q * ps_ref[:, c] * _silu(z)).astype(BF16)
    acc_ref[...] += _dot(y, wout_ref[c, :])

    for r in range(POOL_HIST):
        nps_ref[r] = full(r + nt)
    if g == N_POOL_GROUPS - 1:
        ys = _rmsnorm(acc_ref[...], gf_ref[...])
        for t in range(nt):
            ys_ref[:, t, :] = ys[t * nb:(t + 1) * nb]


def _pool_layer_kernel(x1p_ref, x1s_ref, sp_ref, g_ref, gf_ref, win_hbm, wgrp_hbm, ps_ref, wout_hbm, slot_ref,
                       yp_ref, ys_ref, npp_ref, nps_ref, vh_ref, ybuf_ref, acc_ref, hn_ref,
                       rawa_ref, rawb_ref, win_ref, wgrp_ref, wout_ref, wug_ref, tails_ref, wsem_ref,
                       *, tm, tiles_per_seq, n_prompt_steps, nb, nt, start_pos):
    s = pl.program_id(0)
    raw_refs = (rawa_ref, rawb_ref)
    stage = (raw_refs, slot_ref[0])
    phase = jnp.maximum(s - n_prompt_steps, 0) % N_POOL_GROUPS

    @pl.when(s == 0)
    def _():
        def fuse_group_weights():
            for g in range(N_POOL_GROUPS):
                c = _chunk(g)
                wug_ref[:, c] = _dot(win_ref[:, c], wgrp_ref[c, :]).astype(BF16)

        copies = [pltpu.make_async_copy(src, dst, wsem_ref.at[k]) for k, (src, dst) in
                  enumerate(((win_hbm, win_ref), (wgrp_hbm, wgrp_ref), (wout_hbm, wout_ref)))]
        for copy in copies:
            copy.start()
        copies[0].wait()
        copies[1].wait()
        fuse_group_weights()
        copies[2].wait()

    @pl.when(s < n_prompt_steps)
    def _():
        i = s % tiles_per_seq
        _pool_prompt_tile(i, s // tiles_per_seq, tm, x1p_ref, g_ref, gf_ref, win_ref, wug_ref, ps_ref, wout_ref,
                          yp_ref, tails_ref, vh_ref, ybuf_ref, stage, i == tiles_per_seq - 1)

    @pl.when(s == n_prompt_steps)
    def _():
        _pool_prompt_state(tails_ref, win_ref, npp_ref)

    for g in range(N_POOL_GROUPS):
        @pl.when(jnp.logical_and(s >= n_prompt_steps, phase == g))
        def _(g=g):
            _pool_sample_step(g, nb, nt, start_pos, x1s_ref, sp_ref, g_ref, gf_ref, win_ref, wgrp_ref,
                              ps_ref, wout_ref, ys_ref, nps_ref, acc_ref, hn_ref, stage)


def _resident(shape):
    zeros = (0,) * len(shape)
    return pl.BlockSpec(shape, lambda s: zeros, pipeline_mode=pl.Buffered(1))


def _layer_calls(xp, xs, sc, sp, g, gf, win0, cw, cb, wout0, win1, wgrp, ps, wout1):
    B, S, D = xp.shape
    NB, NT, _ = xs.shape
    E = EXPAND_WIDTH
    tm = PROMPT_TILE
    assert S % tm == 0 and tm >= POOL_CARRY_ROWS and NB % SAMPLE_SPLIT == 0
    tiles = S // tm
    n_prompt = B * tiles
    nb = NB // SAMPLE_SPLIT
    m = NT * nb
    assert nb % SUBLANES == 0 and m <= tm

    def prompt_tile(s):
        sc_ = jnp.minimum(s, n_prompt - 1)
        return (sc_ // tiles, sc_ % tiles, 0)

    def prompt_seq(s):
        return (jnp.minimum(s, n_prompt - 1) // tiles, 0, 0)

    def conv_part(s):
        return (jnp.maximum(s - n_prompt, 0), 0, 0)

    def pool_part(s):
        return (jnp.maximum(s - n_prompt, 0) // N_POOL_GROUPS, 0, 0)

    def pool_group(s):
        q = jnp.maximum(s - n_prompt, 0)
        return (0, q // N_POOL_GROUPS, q % N_POOL_GROUPS)

    tile_bytes = 4 * tm * D * 4
    temp_bytes = 16 * tm * CHUNK * 4 + tm * E * 2
    stage_bytes = 2 * 4 * tm * CHUNK * 4
    headroom = 4 << 20
    stage_slot = jnp.zeros((1,), jnp.int32)
    hbm = pl.BlockSpec(memory_space=pl.ANY)
    conv_planes, pool_planes = 4, 3

    conv_sample_bytes = 2 * 4 * (2 * NT * nb * D + 2 * (CONV_WIDTH - 1) * nb * E)
    next_bytes = 2 * (4 + 2) * (win1.size + wgrp.size + wout1.size) // n_prompt
    conv_limit = min(2 * (win0.size + wout0.size) + tile_bytes + temp_bytes + stage_bytes + next_bytes
                     + conv_sample_bytes + headroom, VMEM_PHYSICAL_BYTES - headroom)
    next_weights = (win1, wgrp, wout1)
    assert all(w.shape[0] % (n_prompt * 2 * SUBLANES) == 0 for w in next_weights)

    def slab(w):
        return pl.BlockSpec((w.shape[0] // n_prompt, w.shape[1]), lambda s: (jnp.minimum(s, n_prompt - 1), 0))

    x1p, x1s, ncp, ncs, win1, wgrp, wout1 = pl.pallas_call(
        functools.partial(_conv_layer_kernel, tm=tm, tiles_per_seq=tiles, n_prompt_steps=n_prompt,
                          nb=nb, nt=NT),
        grid=(n_prompt + SAMPLE_SPLIT,),
        in_specs=[
            pl.BlockSpec((None, tm, D), prompt_tile),
            pl.BlockSpec((nb, NT, D), conv_part),
            pl.BlockSpec((nb, CONV_WIDTH - 1, E), conv_part),
            _resident(g.shape), hbm, _resident(cw.shape), _resident(cb.shape), hbm,
            pl.BlockSpec(memory_space=pltpu.SMEM),
        ] + [slab(w) for w in next_weights],
        out_specs=[
            pl.BlockSpec((None, tm, D), prompt_tile),
            pl.BlockSpec((nb, NT, D), conv_part),
            pl.BlockSpec((None, CONV_WIDTH - 1, E), prompt_seq),
            pl.BlockSpec((nb, CONV_WIDTH - 1, E), conv_part),
        ] + [slab(w) for w in next_weights],
        out_shape=[
            jax.ShapeDtypeStruct((B, S, D), F32),
            jax.ShapeDtypeStruct((NB, NT, D), F32),
            jax.ShapeDtypeStruct((B, CONV_WIDTH - 1, E), F32),
            jax.ShapeDtypeStruct((NB, CONV_WIDTH - 1, E), F32),
        ] + [jax.ShapeDtypeStruct(w.shape, BF16) for w in next_weights],
        scratch_shapes=[
            pltpu.VMEM((CONV_CARRY_ROWS, E), F32),
            pltpu.VMEM((tm, E), BF16),
            pltpu.VMEM((1, conv_planes, tm, CHUNK), F32),
            pltpu.VMEM((1, conv_planes, tm, CHUNK), F32),
            pltpu.VMEM(win0.shape, BF16),
            pltpu.VMEM(wout0.shape, BF16),
            pltpu.SemaphoreType.DMA((2 * conv_planes,)),
        ],
        compiler_params=pltpu.CompilerParams(
            dimension_semantics=("arbitrary",), vmem_limit_bytes=conv_limit),
        name="conv_layer",
    )(xp, xs, sc, g, win0, cw, cb, wout0, stage_slot, *next_weights)

    pool_sample_bytes = 2 * 4 * (2 * NT * nb * D + 2 * nb * (POOL_HIST + 1) * CHUNK) + m * D * 6
    pool_limit = min(2 * (win1.size + wgrp.size + wout1.size + D * E) + tile_bytes + temp_bytes
                     + stage_bytes + pool_sample_bytes + headroom, VMEM_PHYSICAL_BYTES - headroom)
    yp, ys, npp, nps = pl.pallas_call(
        functools.partial(_pool_layer_kernel, tm=tm, tiles_per_seq=tiles, n_prompt_steps=n_prompt,
                          nb=nb, nt=NT, start_pos=PAST_LEN),
        grid=(n_prompt + SAMPLE_SPLIT * N_POOL_GROUPS,),
        in_specs=[
            pl.BlockSpec((None, tm, D), prompt_tile),
            pl.BlockSpec((nb, NT, D), pool_part),
            pl.BlockSpec((POOL_HIST, nb, CHUNK), pool_group),
            _resident(g.shape), _resident(gf.shape), hbm, hbm, _resident(ps.shape), hbm,
            pl.BlockSpec(memory_space=pltpu.SMEM),
        ],
        out_specs=[
            pl.BlockSpec((None, tm, D), prompt_tile),
            pl.BlockSpec((nb, NT, D), pool_part),
            pl.BlockSpec((POOL_HIST, B, E), lambda s: (0, 0, 0)),
            pl.BlockSpec((POOL_HIST, nb, CHUNK), pool_group),
        ],
        out_shape=[
            jax.ShapeDtypeStruct((B, S, D), F32),
            jax.ShapeDtypeStruct((NB, NT, D), F32),
            jax.ShapeDtypeStruct((POOL_HIST, B, E), F32),
            jax.ShapeDtypeStruct((POOL_HIST, NB, E), F32),
        ],
        scratch_shapes=[
            pltpu.VMEM((POOL_CARRY_ROWS, E), F32),
            pltpu.VMEM((tm, E), BF16),
            pltpu.VMEM((m, D), F32),
            pltpu.VMEM((m, D), BF16),
            pltpu.VMEM((1, pool_planes, tm, CHUNK), F32),
            pltpu.VMEM((1, pool_planes, tm, CHUNK), F32),
            pltpu.VMEM(win1.shape, BF16),
            pltpu.VMEM(wgrp.shape, BF16),
            pltpu.VMEM(wout1.shape, BF16),
            pltpu.VMEM((D, E), BF16),
            pltpu.VMEM((B * POOL_CARRY_ROWS, D), BF16),
            pltpu.SemaphoreType.DMA((2 * pool_planes,)),
        ],
        compiler_params=pltpu.CompilerParams(
            dimension_semantics=("arbitrary",), vmem_limit_bytes=pool_limit),
        name="pool_layer",
    )(x1p, x1s, sp, g, gf, win1, wgrp, ps, wout1, stage_slot)
    return yp, ys, ncp, ncs, npp, nps


def kernel(x_prompt, x_sample, state_conv, state_pool, norm_g, final_norm_g,
           conv_w_in, conv_w, conv_b, conv_w_out,
           pool_w_in, pool_w_grp, pool_scale, pool_w_out):
    assert norm_g.shape[0] == 2 and conv_w_in.shape[0] == 1 and pool_w_in.shape[0] == 1
    gf = final_norm_g.reshape(1, D_MODEL)
    win0, wout0, win1, wout1 = conv_w_in[0], conv_w_out[0], pool_w_in[0], pool_w_out[0]
    wgrp = pool_w_grp[0].reshape(EXPAND_WIDTH, POOL_GROUP_WIDTH)
    yp, ys, ncp, ncs, npp, nps = _layer_calls(
        x_prompt, x_sample, state_conv[0], jnp.transpose(state_pool[0], (1, 0, 2)), norm_g, gf,
        win0, conv_w.reshape(1, CONV_WIDTH * EXPAND_WIDTH), conv_b, wout0, win1, wgrp, pool_scale, wout1)
    return (yp, ys, ncp[None], ncs[None],
            jnp.transpose(npp, (1, 0, 2))[None], jnp.transpose(nps, (1, 0, 2))[None])
```

```python
import functools

import jax
import jax.numpy as jnp
from jax import lax
from jax.experimental import pallas as pl
from jax.experimental.pallas import tpu as pltpu

D_MODEL = 1024
EXPAND_WIDTH = 2048
CONV_WIDTH = 3
POOL_WINDOWS = (2, 4, 8, 16)
N_POOL_GROUPS = len(POOL_WINDOWS)
POOL_GROUP_WIDTH = EXPAND_WIDTH // N_POOL_GROUPS
POOL_HIST = max(POOL_WINDOWS) - 1
PAST_LEN = 16384
RMS_EPS = 1e-6
LOG2_E = 1.4426950408889634

LANES = 128
SUBLANES = 8
CONV_CARRY_ROWS = SUBLANES
POOL_CARRY_ROWS = 2 * SUBLANES
CHUNK = POOL_GROUP_WIDTH
N_CHUNKS = EXPAND_WIDTH // CHUNK
PROMPT_TILE = 512
SAMPLE_SPLIT = 2
VMEM_PHYSICAL_BYTES = 64 * 1024 * 1024

BF16 = jnp.bfloat16
F32 = jnp.float32


def _dot(a, b):
    return jnp.dot(a, b, preferred_element_type=F32)


def _rmsnorm(x, g):
    r = lax.rsqrt(jnp.mean(x * x, axis=-1, keepdims=True) + RMS_EPS)
    return x * r * g


def _silu(z):
    return z / (1.0 + jnp.exp2(z * (-LOG2_E)))


def _chunk(j, section=0):
    lo = section * EXPAND_WIDTH + j * CHUNK
    return slice(lo, lo + CHUNK)


def _weight_pieces(src_hbm, dst_ref, piece):
    pr, pc = piece
    rows, cols = src_hbm.shape
    assert rows % pr == 0 and cols % pc == 0
    pairs = []
    for i in range(rows // pr):
        for j in range(cols // pc):
            r, c = pl.ds(i * pr, pr), pl.ds(j * pc, pc)
            pairs.append((src_hbm.at[r, c], dst_ref.at[r, c]))
    return pairs


def _load_weights_as_bf16(weights, raw_refs, sem_ref):
    slots = [r.at[0, k] for r in raw_refs for k in range(r.shape[1])]
    pieces = []
    for src_hbm, dst_ref in weights:
        pieces += _weight_pieces(src_hbm, dst_ref, slots[0].shape)
    n, r = len(pieces), len(slots)

    def copy(k):
        return pltpu.make_async_copy(pieces[k][0], slots[k % r], sem_ref.at[k % r])

    for k in range(min(r, n)):
        copy(k).start()
    for k in range(n):
        copy(k).wait()
        pieces[k][1][...] = slots[k % r][...].astype(BF16)
        if k + r < n:
            copy(k + r).start()


def _time_major(ref, nt):
    return jnp.concatenate([ref[:, t, :] for t in range(nt)], axis=0)


def _row_parts(m, parts):
    return [slice(h * (m // parts), (h + 1) * (m // parts)) for h in range(parts)]


def _conv_gates(hn, win_ref, j, stage, parts=1):
    raw_refs, slot = stage
    raw_ref = raw_refs[j % len(raw_refs)]
    m = hn.shape[0]
    for r in _row_parts(m, parts):
        for sec in range(4):
            raw_ref[slot, sec, r, :] = _dot(hn[r], win_ref[:, _chunk(j, sec)])
    gb, gc, v, z = (raw_ref[slot, sec, 0:m, :] for sec in range(4))
    return gb, gc * v, z


def _conv_prompt_tile(i, tm, x_ref, g_ref, win_ref, cw_ref, cb_ref, wout_ref,
                      x1_ref, nconv_ref, cvh_ref, ybuf_ref, stage, last_tile):
    @pl.when(i == 0)
    def _():
        cvh_ref[...] = jnp.zeros_like(cvh_ref)

    x = x_ref[...]
    hn = _rmsnorm(x, g_ref[0:1, :]).astype(BF16)
    for j in range(N_CHUNKS):
        c = _chunk(j)
        gb, cv, z = _conv_gates(hn, win_ref, j, stage, parts=2 if j == 0 else 1)
        ext = jnp.concatenate([cvh_ref[:, c], cv], axis=0)
        cm1 = pltpu.roll(ext, 1, 0)[CONV_CARRY_ROWS:]
        cm2 = pltpu.roll(ext, 2, 0)[CONV_CARRY_ROWS:]
        conv = cb_ref[:, c] + cm2 * cw_ref[:, _chunk(j, 0)]
        conv = conv + cm1 * cw_ref[:, _chunk(j, 1)]
        conv = conv + cv * cw_ref[:, _chunk(j, 2)]
        ybuf_ref[0:tm, c] = (gb * conv * _silu(z)).astype(BF16)
        cvh_ref[:, c] = cv[tm - CONV_CARRY_ROWS:]
    for r in _row_parts(tm, 2):
        x1_ref[r, :] = x[r] + _dot(ybuf_ref[r, :], wout_ref[...])

    @pl.when(last_tile)
    def _():
        nconv_ref[...] = cvh_ref[CONV_CARRY_ROWS - (CONV_WIDTH - 1):, :]


def _conv_sample_step(nb, nt, xs_ref, sc_ref, g_ref, win_ref, cw_ref, cb_ref, wout_ref,
                      x1s_ref, ncs_ref, ybuf_ref, stage):
    m = nt * nb
    xs = _time_major(xs_ref, nt)
    hn = _rmsnorm(xs, g_ref[0:1, :]).astype(BF16)
    for j in range(N_CHUNKS):
        c = _chunk(j)
        gb, cv, z = _conv_gates(hn, win_ref, j, stage)
        full = jnp.concatenate([sc_ref[:, k, c] for k in range(CONV_WIDTH - 1)] + [cv], axis=0)
        conv = cb_ref[:, c]
        for k in range(CONV_WIDTH):
            conv = conv + full[k * nb:(k + nt) * nb] * cw_ref[:, _chunk(j, k)]
        ybuf_ref[0:m, c] = (gb * conv * _silu(z)).astype(BF16)
        for k in range(CONV_WIDTH - 1):
            ncs_ref[:, k, c] = full[(nt + k) * nb:(nt + k + 1) * nb]
    x1 = xs + _dot(ybuf_ref[0:m, :], wout_ref[...])
    for t in range(nt):
        x1s_ref[:, t, :] = x1[t * nb:(t + 1) * nb]


def _conv_layer_kernel(xp_ref, xs_ref, sc_ref, g_ref, win_hbm, cw_ref, cb_ref, wout_hbm, slot_ref,
                       nxt_in_ref, nxt_grp_ref, nxt_out_ref,
                       x1p_ref, x1s_ref, ncp_ref, ncs_ref, nxt_in_bf_ref, nxt_grp_bf_ref, nxt_out_bf_ref,
                       cvh_ref, ybuf_ref, rawa_ref, rawb_ref, win_ref, wout_ref, wsem_ref,
                       *, tm, tiles_per_seq, n_prompt_steps, nb, nt):
    s = pl.program_id(0)
    raw_refs = (rawa_ref, rawb_ref)
    stage = (raw_refs, slot_ref[0])

    @pl.when(s == 0)
    def _():
        _load_weights_as_bf16([(win_hbm, win_ref), (wout_hbm, wout_ref)], raw_refs, wsem_ref)

    @pl.when(s < n_prompt_steps)
    def _():
        i = s % tiles_per_seq
        _conv_prompt_tile(i, tm, xp_ref, g_ref, win_ref, cw_ref, cb_ref, wout_ref,
                          x1p_ref, ncp_ref, cvh_ref, ybuf_ref, stage, i == tiles_per_seq - 1)
        for src_ref, dst_ref in ((nxt_in_ref, nxt_in_bf_ref), (nxt_grp_ref, nxt_grp_bf_ref),
                                 (nxt_out_ref, nxt_out_bf_ref)):
            dst_ref[...] = src_ref[...].astype(BF16)

    @pl.when(s >= n_prompt_steps)
    def _():
        _conv_sample_step(nb, nt, xs_ref, sc_ref, g_ref, win_ref, cw_ref, cb_ref, wout_ref,
                          x1s_ref, ncs_ref, ybuf_ref, stage)


def _pool_prompt_tile(i, seq, tm, x1_ref, g_ref, gf_ref, win_ref, wug_ref, ps_ref, wout_ref,
                      y_ref, tails_ref, vh_ref, ybuf_ref, stage, last_tile):
    @pl.when(i == 0)
    def _():
        vh_ref[...] = jnp.zeros_like(vh_ref)

    x1 = x1_ref[...]
    hn1 = _rmsnorm(x1, g_ref[1:2, :]).astype(BF16)
    seen = (i * tm + 1 + lax.broadcasted_iota(jnp.int32, (tm, LANES), 0)).astype(F32)
    raw_refs, slot = stage
    for g, w in enumerate(POOL_WINDOWS):
        c = _chunk(g)
        raw_ref = raw_refs[g % len(raw_refs)]
        for r in _row_parts(tm, 2 if g == 0 else 1):
            raw_ref[slot, 0, r, :] = _dot(hn1[r], wug_ref[:, c])
            raw_ref[slot, 1, r, :] = _dot(hn1[r], win_ref[:, _chunk(g, 1)])
        v = raw_ref[slot, 0]
        s = jnp.concatenate([vh_ref[:, c], v], axis=0)
        shift = 1
        while shift < w:
            s = s + pltpu.roll(s, shift, 0)
            shift *= 2
        inv = 1.0 / jnp.minimum(jnp.float32(w), seen)
        inv = jnp.concatenate([inv] * (CHUNK // LANES), axis=1)
        q = s[POOL_CARRY_ROWS:] * inv - v
        z = raw_ref[slot, 1]
        ybuf_ref[0:tm, c] = (q * ps_ref[:, c] * _silu(z)).astype(BF16)
        vh_ref[:, c] = v[tm - POOL_CARRY_ROWS:]
    kh = EXPAND_WIDTH // 2
    rows = _row_parts(tm, 2)
    lo = [_dot(ybuf_ref[r, 0:kh], wout_ref[0:kh, :]) for r in rows]
    hi = [_dot(ybuf_ref[r, kh:], wout_ref[kh:, :]) for r in rows]
    for r, a, b in zip(rows, lo, hi):
        y_ref[r, :] = _rmsnorm(x1_ref[r, :] + (a + b), gf_ref[...])

    @pl.when(last_tile)
    def _():
        tail = slice(tm - POOL_CARRY_ROWS, tm)
        at = pl.multiple_of(seq * POOL_CARRY_ROWS, POOL_CARRY_ROWS)
        tails_ref[pl.ds(at, POOL_CARRY_ROWS), :] = _rmsnorm(x1_ref[tail, :], g_ref[1:2, :]).astype(BF16)


def _pool_prompt_state(tails_ref, win_ref, npool_ref):
    u_tails = _dot(tails_ref[...], win_ref[:, 0:EXPAND_WIDTH])
    for b in range(npool_ref.shape[1]):
        npool_ref[:, b, :] = u_tails[(b + 1) * POOL_CARRY_ROWS - POOL_HIST:(b + 1) * POOL_CARRY_ROWS, :]


def _pool_sample_step(g, nb, nt, start_pos, x1s_ref, sp_ref, g_ref, gf_ref, win_ref, wgrp_ref, ps_ref,
                      wout_ref, ys_ref, nps_ref, acc_ref, hn_ref, stage):
    w = POOL_WINDOWS[g]
    c = _chunk(g)
    m = nt * nb
    if g == 0:
        x1 = _time_major(x1s_ref, nt)
        acc_ref[...] = x1
        hn_ref[...] = _rmsnorm(x1, g_ref[1:2, :]).astype(BF16)
    hn1 = hn_ref[...]
    raw_refs, slot = stage
    raw_ref = raw_refs[g % len(raw_refs)]
    raw_ref[slot, 0, 0:m, :] = _dot(hn1, win_ref[:, _chunk(g, 0)])
    raw_ref[slot, 1, 0:m, :] = _dot(hn1, win_ref[:, _chunk(g, 1)])
    u = raw_ref[slot, 0, 0:m, :]

    def full(r):
        if r < POOL_HIST:
            return sp_ref[r]
        return u[(r - POOL_HIST) * nb:(r - POOL_HIST + 1) * nb]

    parts = []
    for t in range(nt):
        win = full(POOL_HIST + t)
        for k in range(1, w):
            win = win + full(POOL_HIST + t - k)
        inv = 1.0 / float(min(w, start_pos + t + 1))
        parts.append(win * inv - full(POOL_HIST + t))
    p = jnp.concatenate(parts, axis=0).astype(BF16)
    raw_ref[slot, 2, 0:m, :] = _dot(p, wgrp_ref[_chunk(g), :])
    q, z = raw_ref[slot, 2, 0:m, :], raw_ref[slot, 1, 0:m, :]
    y = (q * ps_ref[:, c] * _silu(z)).astype(BF16)
    acc_ref[...] += _dot(y, wout_ref[c, :])

    for r in range(POOL_HIST):
        nps_ref[r] = full(r + nt)
    if g == N_POOL_GROUPS - 1:
        ys = _rmsnorm(acc_ref[...], gf_ref[...])
        for t in range(nt):
            ys_ref[:, t, :] = ys[t * nb:(t + 1) * nb]


def _pool_layer_kernel(x1p_ref, x1s_ref, sp_ref, g_ref, gf_ref, win_hbm, wgrp_hbm, ps_ref, wout_hbm, slot_ref,
                       yp_ref, ys_ref, npp_ref, nps_ref, vh_ref, ybuf_ref, acc_ref, hn_ref,
                       rawa_ref, rawb_ref, win_ref, wgrp_ref, wout_ref, wug_ref, tails_ref, wsem_ref,
                       *, tm, tiles_per_seq, n_prompt_steps, nb, nt, start_pos):
    s = pl.program_id(0)
    raw_refs = (rawa_ref, rawb_ref)
    stage = (raw_refs, slot_ref[0])
    phase = jnp.maximum(s - n_prompt_steps, 0) % N_POOL_GROUPS

    @pl.when(s == 0)
    def _():
        def fuse_group_weights():
            for g in range(N_POOL_GROUPS):
                c = _chunk(g)
                wug_ref[:, c] = _dot(win_ref[:, c], wgrp_ref[c, :]).astype(BF16)

        u_cols, z_cols = pl.ds(0, EXPAND_WIDTH), pl.ds(EXPAND_WIDTH, EXPAND_WIDTH)
        copies = [pltpu.make_async_copy(src, dst, wsem_ref.at[k]) for k, (src, dst) in
                  enumerate(((win_hbm.at[:, u_cols], win_ref.at[:, u_cols]), (wgrp_hbm, wgrp_ref),
                             (win_hbm.at[:, z_cols], win_ref.at[:, z_cols]), (wout_hbm, wout_ref)))]
        for copy in copies:
            copy.start()
        copies[0].wait()
        copies[1].wait()
        fuse_group_weights()
        copies[2].wait()
        copies[3].wait()

    @pl.when(s < n_prompt_steps)
    def _():
        i = s % tiles_per_seq
        _pool_prompt_tile(i, s // tiles_per_seq, tm, x1p_ref, g_ref, gf_ref, win_ref, wug_ref, ps_ref, wout_ref,
                          yp_ref, tails_ref, vh_ref, ybuf_ref, stage, i == tiles_per_seq - 1)

    @pl.when(s == n_prompt_steps)
    def _():
        _pool_prompt_state(tails_ref, win_ref, npp_ref)

    for g in range(N_POOL_GROUPS):
        @pl.when(jnp.logical_and(s >= n_prompt_steps, phase == g))
        def _(g=g):
            _pool_sample_step(g, nb, nt, start_pos, x1s_ref, sp_ref, g_ref, gf_ref, win_ref, wgrp_ref,
                              ps_ref, wout_ref, ys_ref, nps_ref, acc_ref, hn_ref, stage)


def _resident(shape):
    zeros = (0,) * len(shape)
    return pl.BlockSpec(shape, lambda s: zeros, pipeline_mode=pl.Buffered(1))


def _layer_calls(xp, xs, sc, sp, g, gf, win0, cw, cb, wout0, win1, wgrp, ps, wout1):
    B, S, D = xp.shape
    NB, NT, _ = xs.shape
    E = EXPAND_WIDTH
    tm = PROMPT_TILE
    assert S % tm == 0 and tm >= POOL_CARRY_ROWS and NB % SAMPLE_SPLIT == 0
    tiles = S // tm
    n_prompt = B * tiles
    nb = NB // SAMPLE_SPLIT
    m = NT * nb
    assert nb % SUBLANES == 0 and m <= tm

    def prompt_tile(s):
        sc_ = jnp.minimum(s, n_prompt - 1)
        return (sc_ // tiles, sc_ % tiles, 0)

    def prompt_seq(s):
        return (jnp.minimum(s, n_prompt - 1) // tiles, 0, 0)

    def conv_part(s):
        return (jnp.maximum(s - n_prompt, 0), 0, 0)

    def pool_part(s):
        return (jnp.maximum(s - n_prompt, 0) // N_POOL_GROUPS, 0, 0)

    def pool_group(s):
        q = jnp.maximum(s - n_prompt, 0)
        return (0, q // N_POOL_GROUPS, q % N_POOL_GROUPS)

    tile_bytes = 4 * tm * D * 4
    temp_bytes = 16 * tm * CHUNK * 4 + tm * E * 2
    stage_bytes = 2 * 4 * tm * CHUNK * 4
    headroom = 4 << 20
    stage_slot = jnp.zeros((1,), jnp.int32)
    hbm = pl.BlockSpec(memory_space=pl.ANY)
    conv_planes, pool_planes = 4, 3

    conv_sample_bytes = 2 * 4 * (2 * NT * nb * D + 2 * (CONV_WIDTH - 1) * nb * E)
    next_bytes = 2 * (4 + 2) * (win1.size + wgrp.size + wout1.size) // n_prompt
    conv_limit = min(2 * (win0.size + wout0.size) + tile_bytes + temp_bytes + stage_bytes + next_bytes
                     + conv_sample_bytes + headroom, VMEM_PHYSICAL_BYTES - headroom)
    next_weights = (win1, wgrp, wout1)
    assert all(w.shape[0] % (n_prompt * 2 * SUBLANES) == 0 for w in next_weights)

    def slab(w):
        return pl.BlockSpec((w.shape[0] // n_prompt, w.shape[1]), lambda s: (jnp.minimum(s, n_prompt - 1), 0))

    x1p, x1s, ncp, ncs, win1, wgrp, wout1 = pl.pallas_call(
        functools.partial(_conv_layer_kernel, tm=tm, tiles_per_seq=tiles, n_prompt_steps=n_prompt,
                          nb=nb, nt=NT),
        grid=(n_prompt + SAMPLE_SPLIT,),
        in_specs=[
            pl.BlockSpec((None, tm, D), prompt_tile),
            pl.BlockSpec((nb, NT, D), conv_part),
            pl.BlockSpec((nb, CONV_WIDTH - 1, E), conv_part),
            _resident(g.shape), hbm, _resident(cw.shape), _resident(cb.shape), hbm,
            pl.BlockSpec(memory_space=pltpu.SMEM),
        ] + [slab(w) for w in next_weights],
        out_specs=[
            pl.BlockSpec((None, tm, D), prompt_tile),
            pl.BlockSpec((nb, NT, D), conv_part),
            pl.BlockSpec((None, CONV_WIDTH - 1, E), prompt_seq),
            pl.BlockSpec((nb, CONV_WIDTH - 1, E), conv_part),
        ] + [slab(w) for w in next_weights],
        out_shape=[
            jax.ShapeDtypeStruct((B, S, D), F32),
            jax.ShapeDtypeStruct((NB, NT, D), F32),
            jax.ShapeDtypeStruct((B, CONV_WIDTH - 1, E), F32),
            jax.ShapeDtypeStruct((NB, CONV_WIDTH - 1, E), F32),
        ] + [jax.ShapeDtypeStruct(w.shape, BF16) for w in next_weights],
        scratch_shapes=[
            pltpu.VMEM((CONV_CARRY_ROWS, E), F32),
            pltpu.VMEM((tm, E), BF16),
            pltpu.VMEM((1, conv_planes, tm, CHUNK), F32),
            pltpu.VMEM((1, conv_planes, tm, CHUNK), F32),
            pltpu.VMEM(win0.shape, BF16),
            pltpu.VMEM(wout0.shape, BF16),
            pltpu.SemaphoreType.DMA((2 * conv_planes,)),
        ],
        compiler_params=pltpu.CompilerParams(
            dimension_semantics=("arbitrary",), vmem_limit_bytes=conv_limit),
        name="conv_layer",
    )(xp, xs, sc, g, win0, cw, cb, wout0, stage_slot, *next_weights)

    pool_sample_bytes = 2 * 4 * (2 * NT * nb * D + 2 * nb * (POOL_HIST + 1) * CHUNK) + m * D * 6
    pool_limit = min(2 * (win1.size + wgrp.size + wout1.size + D * E) + tile_bytes + temp_bytes
                     + stage_bytes + pool_sample_bytes + headroom, VMEM_PHYSICAL_BYTES - headroom)
    yp, ys, npp, nps = pl.pallas_call(
        functools.partial(_pool_layer_kernel, tm=tm, tiles_per_seq=tiles, n_prompt_steps=n_prompt,
                          nb=nb, nt=NT, start_pos=PAST_LEN),
        grid=(n_prompt + SAMPLE_SPLIT * N_POOL_GROUPS,),
        in_specs=[
            pl.BlockSpec((None, tm, D), prompt_tile),
            pl.BlockSpec((nb, NT, D), pool_part),
            pl.BlockSpec((POOL_HIST, nb, CHUNK), pool_group),
            _resident(g.shape), _resident(gf.shape), hbm, hbm, _resident(ps.shape), hbm,
            pl.BlockSpec(memory_space=pltpu.SMEM),
        ],
        out_specs=[
            pl.BlockSpec((None, tm, D), prompt_tile),
            pl.BlockSpec((nb, NT, D), pool_part),
            pl.BlockSpec((POOL_HIST, B, E), lambda s: (0, 0, 0)),
            pl.BlockSpec((POOL_HIST, nb, CHUNK), pool_group),
        ],
        out_shape=[
            jax.ShapeDtypeStruct((B, S, D), F32),
            jax.ShapeDtypeStruct((NB, NT, D), F32),
            jax.ShapeDtypeStruct((POOL_HIST, B, E), F32),
            jax.ShapeDtypeStruct((POOL_HIST, NB, E), F32),
        ],
        scratch_shapes=[
            pltpu.VMEM((POOL_CARRY_ROWS, E), F32),
            pltpu.VMEM((tm, E), BF16),
            pltpu.VMEM((m, D), F32),
            pltpu.VMEM((m, D), BF16),
            pltpu.VMEM((1, pool_planes, tm, CHUNK), F32),
            pltpu.VMEM((1, pool_planes, tm, CHUNK), F32),
            pltpu.VMEM(win1.shape, BF16),
            pltpu.VMEM(wgrp.shape, BF16),
            pltpu.VMEM(wout1.shape, BF16),
            pltpu.VMEM((D, E), BF16),
            pltpu.VMEM((B * POOL_CARRY_ROWS, D), BF16),
            pltpu.SemaphoreType.DMA((2 * pool_planes,)),
        ],
        compiler_params=pltpu.CompilerParams(
            dimension_semantics=("arbitrary",), vmem_limit_bytes=pool_limit),
        name="pool_layer",
    )(x1p, x1s, sp, g, gf, win1, wgrp, ps, wout1, stage_slot)
    return yp, ys, ncp, ncs, npp, nps


def kernel(x_prompt, x_sample, state_conv, state_pool, norm_g, final_norm_g,
           conv_w_in, conv_w, conv_b, conv_w_out,
           pool_w_in, pool_w_grp, pool_scale, pool_w_out):
    assert norm_g.shape[0] == 2 and conv_w_in.shape[0] == 1 and pool_w_in.shape[0] == 1
    gf = final_norm_g.reshape(1, D_MODEL)
    win0, wout0, win1, wout1 = conv_w_in[0], conv_w_out[0], pool_w_in[0], pool_w_out[0]
    wgrp = pool_w_grp[0].reshape(EXPAND_WIDTH, POOL_GROUP_WIDTH)
    yp, ys, ncp, ncs, npp, nps = _layer_calls(
        x_prompt, x_sample, state_conv[0], jnp.transpose(state_pool[0], (1, 0, 2)), norm_g, gf,
        win0, conv_w.reshape(1, CONV_WIDTH * EXPAND_WIDTH), conv_b, wout0, win1, wgrp, pool_scale, wout1)
    return (yp, ys, ncp[None], ncs[None],
            jnp.transpose(npp, (1, 0, 2))[None], jnp.transpose(nps, (1, 0, 2))[None])
```

```python
import functools

import jax
import jax.numpy as jnp
from jax import lax
from jax.experimental import pallas as pl
from jax.experimental.pallas import tpu as pltpu

D_MODEL = 1024
EXPAND_WIDTH = 2048
CONV_WIDTH = 3
POOL_WINDOWS = (2, 4, 8, 16)
N_POOL_GROUPS = len(POOL_WINDOWS)
POOL_GROUP_WIDTH = EXPAND_WIDTH // N_POOL_GROUPS
POOL_HIST = max(POOL_WINDOWS) - 1
PAST_LEN = 16384
RMS_EPS = 1e-6
LOG2_E = 1.4426950408889634

LANES = 128
SUBLANES = 8
CONV_CARRY_ROWS = SUBLANES
POOL_CARRY_ROWS = 2 * SUBLANES
CHUNK = POOL_GROUP_WIDTH
N_CHUNKS = EXPAND_WIDTH // CHUNK
PROMPT_TILE = 512
SAMPLE_SPLIT = 2
VMEM_PHYSICAL_BYTES = 64 * 1024 * 1024

BF16 = jnp.bfloat16
F32 = jnp.float32


def _dot(a, b):
    return jnp.dot(a, b, preferred_element_type=F32)


def _rmsnorm(x, g):
    r = lax.rsqrt(jnp.mean(x * x, axis=-1, keepdims=True) + RMS_EPS)
    return x * r * g


def _silu(z):
    return z / (1.0 + jnp.exp2(z * (-LOG2_E)))


def _chunk(j, section=0):
    lo = section * EXPAND_WIDTH + j * CHUNK
    return slice(lo, lo + CHUNK)


def _weight_pieces(src_hbm, dst_ref, piece):
    pr, pc = piece
    rows, cols = src_hbm.shape
    assert rows % pr == 0 and cols % pc == 0
    pairs = []
    for i in range(rows // pr):
        for j in range(cols // pc):
            r, c = pl.ds(i * pr, pr), pl.ds(j * pc, pc)
            pairs.append((src_hbm.at[r, c], dst_ref.at[r, c]))
    return pairs


def _load_weights_as_bf16(weights, raw_refs, sem_ref):
    slots = [r.at[0, k] for r in raw_refs for k in range(r.shape[1])]
    pieces = []
    for src_hbm, dst_ref in weights:
        pieces += _weight_pieces(src_hbm, dst_ref, slots[0].shape)
    n, r = len(pieces), len(slots)

    def copy(k):
        return pltpu.make_async_copy(pieces[k][0], slots[k % r], sem_ref.at[k % r])

    for k in range(min(r, n)):
        copy(k).start()
    for k in range(n):
        copy(k).wait()
        pieces[k][1][...] = slots[k % r][...].astype(BF16)
        if k + r < n:
            copy(k + r).start()


def _time_major(ref, nt):
    return jnp.concatenate([ref[:, t, :] for t in range(nt)], axis=0)


def _row_parts(m, parts):
    return [slice(h * (m // parts), (h + 1) * (m // parts)) for h in range(parts)]


def _conv_gates(hn, win_ref, j, stage, parts=1):
    raw_refs, slot = stage
    raw_ref = raw_refs[j % len(raw_refs)]
    m = hn.shape[0]
    for r in _row_parts(m, parts):
        for sec in range(4):
            raw_ref[slot, sec, r, :] = _dot(hn[r], win_ref[:, _chunk(j, sec)])
    gb, gc, v, z = (raw_ref[slot, sec, 0:m, :] for sec in range(4))
    return gb, gc * v, z


def _conv_prompt_tile(i, tm, x_ref, g_ref, win_ref, cw_ref, cb_ref, wout_ref,
                      x1_ref, hn1_ref, nconv_ref, cvh_ref, ybuf_ref, stage, last_tile):
    @pl.when(i == 0)
    def _():
        cvh_ref[...] = jnp.zeros_like(cvh_ref)

    x = x_ref[...]
    hn = _rmsnorm(x, g_ref[0:1, :]).astype(BF16)
    for j in range(N_CHUNKS):
        c = _chunk(j)
        gb, cv, z = _conv_gates(hn, win_ref, j, stage, parts=2 if j == 0 else 1)
        ext = jnp.concatenate([cvh_ref[:, c], cv], axis=0)
        cm1 = pltpu.roll(ext, 1, 0)[CONV_CARRY_ROWS:]
        cm2 = pltpu.roll(ext, 2, 0)[CONV_CARRY_ROWS:]
        conv = cb_ref[:, c] + cm2 * cw_ref[:, _chunk(j, 0)]
        conv = conv + cm1 * cw_ref[:, _chunk(j, 1)]
        conv = conv + cv * cw_ref[:, _chunk(j, 2)]
        ybuf_ref[0:tm, c] = (gb * conv * _silu(z)).astype(BF16)
        cvh_ref[:, c] = cv[tm - CONV_CARRY_ROWS:]
    for r in _row_parts(tm, 2):
        x1 = x[r] + _dot(ybuf_ref[r, :], wout_ref[...])
        x1_ref[r, :] = x1
        hn1_ref[r, :] = _rmsnorm(x1, g_ref[1:2, :]).astype(BF16)

    @pl.when(last_tile)
    def _():
        nconv_ref[...] = cvh_ref[CONV_CARRY_ROWS - (CONV_WIDTH - 1):, :]


def _conv_sample_step(nb, nt, xs_ref, sc_ref, g_ref, win_ref, cw_ref, cb_ref, wout_ref,
                      x1s_ref, ncs_ref, ybuf_ref, stage):
    m = nt * nb
    xs = _time_major(xs_ref, nt)
    hn = _rmsnorm(xs, g_ref[0:1, :]).astype(BF16)
    for j in range(N_CHUNKS):
        c = _chunk(j)
        gb, cv, z = _conv_gates(hn, win_ref, j, stage)
        full = jnp.concatenate([sc_ref[:, k, c] for k in range(CONV_WIDTH - 1)] + [cv], axis=0)
        conv = cb_ref[:, c]
        for k in range(CONV_WIDTH):
            conv = conv + full[k * nb:(k + nt) * nb] * cw_ref[:, _chunk(j, k)]
        ybuf_ref[0:m, c] = (gb * conv * _silu(z)).astype(BF16)
        for k in range(CONV_WIDTH - 1):
            ncs_ref[:, k, c] = full[(nt + k) * nb:(nt + k + 1) * nb]
    x1 = xs + _dot(ybuf_ref[0:m, :], wout_ref[...])
    for t in range(nt):
        x1s_ref[:, t, :] = x1[t * nb:(t + 1) * nb]


def _conv_layer_kernel(xp_ref, xs_ref, sc_ref, g_ref, win_hbm, cw_ref, cb_ref, wout_hbm, slot_ref,
                       nxt_in_ref, nxt_grp_ref, nxt_out_ref,
                       x1p_ref, x1s_ref, ncp_ref, ncs_ref, nxt_in_bf_ref, nxt_grp_bf_ref, nxt_out_bf_ref, hn1p_ref,
                       cvh_ref, ybuf_ref, rawa_ref, rawb_ref, win_ref, wout_ref, wsem_ref,
                       *, tm, tiles_per_seq, n_prompt_steps, nb, nt):
    s = pl.program_id(0)
    raw_refs = (rawa_ref, rawb_ref)
    stage = (raw_refs, slot_ref[0])

    @pl.when(s == 0)
    def _():
        _load_weights_as_bf16([(win_hbm, win_ref), (wout_hbm, wout_ref)], raw_refs, wsem_ref)

    @pl.when(s < n_prompt_steps)
    def _():
        i = s % tiles_per_seq
        _conv_prompt_tile(i, tm, xp_ref, g_ref, win_ref, cw_ref, cb_ref, wout_ref,
                          x1p_ref, hn1p_ref, ncp_ref, cvh_ref, ybuf_ref, stage, i == tiles_per_seq - 1)
        for src_ref, dst_ref in ((nxt_in_ref, nxt_in_bf_ref), (nxt_grp_ref, nxt_grp_bf_ref),
                                 (nxt_out_ref, nxt_out_bf_ref)):
            dst_ref[...] = src_ref[...].astype(BF16)

    @pl.when(s >= n_prompt_steps)
    def _():
        _conv_sample_step(nb, nt, xs_ref, sc_ref, g_ref, win_ref, cw_ref, cb_ref, wout_ref,
                          x1s_ref, ncs_ref, ybuf_ref, stage)


def _pool_prompt_tile(i, seq, tm, x1_ref, hn1_ref, gf_ref, win_ref, wug_ref, ps_ref, wout_ref,
                      y_ref, tails_ref, vh_ref, ybuf_ref, stage, last_tile):
    @pl.when(i == 0)
    def _():
        vh_ref[...] = jnp.zeros_like(vh_ref)

    hn1 = hn1_ref[...]
    seen = (i * tm + 1 + lax.broadcasted_iota(jnp.int32, (tm, LANES), 0)).astype(F32)
    raw_refs, slot = stage
    for g, w in enumerate(POOL_WINDOWS):
        c = _chunk(g)
        raw_ref = raw_refs[g % len(raw_refs)]
        for r in _row_parts(tm, 2 if g == 0 else 1):
            raw_ref[slot, 0, r, :] = _dot(hn1[r], wug_ref[:, c])
            raw_ref[slot, 1, r, :] = _dot(hn1[r], win_ref[:, _chunk(g, 1)])
        v = raw_ref[slot, 0]
        s = jnp.concatenate([vh_ref[:, c], v], axis=0)
        shift = 1
        while shift < w:
            s = s + pltpu.roll(s, shift, 0)
            shift *= 2
        inv = 1.0 / jnp.minimum(jnp.float32(w), seen)
        inv = jnp.concatenate([inv] * (CHUNK // LANES), axis=1)
        q = s[POOL_CARRY_ROWS:] * inv - v
        z = raw_ref[slot, 1]
        ybuf_ref[0:tm, c] = (q * ps_ref[:, c] * _silu(z)).astype(BF16)
        vh_ref[:, c] = v[tm - POOL_CARRY_ROWS:]
    kh = EXPAND_WIDTH // 2
    rows = _row_parts(tm, 2)
    lo = [_dot(ybuf_ref[r, 0:kh], wout_ref[0:kh, :]) for r in rows]
    hi = [_dot(ybuf_ref[r, kh:], wout_ref[kh:, :]) for r in rows]
    for r, a, b in zip(rows, lo, hi):
        y_ref[r, :] = _rmsnorm(x1_ref[r, :] + (a + b), gf_ref[...])

    @pl.when(last_tile)
    def _():
        tail = slice(tm - POOL_CARRY_ROWS, tm)
        at = pl.multiple_of(seq * POOL_CARRY_ROWS, POOL_CARRY_ROWS)
        tails_ref[pl.ds(at, POOL_CARRY_ROWS), :] = hn1_ref[tail, :]


def _pool_prompt_state(tails_ref, win_ref, npool_ref):
    u_tails = _dot(tails_ref[...], win_ref[:, 0:EXPAND_WIDTH])
    for b in range(npool_ref.shape[1]):
        npool_ref[:, b, :] = u_tails[(b + 1) * POOL_CARRY_ROWS - POOL_HIST:(b + 1) * POOL_CARRY_ROWS, :]


def _pool_sample_step(g, nb, nt, start_pos, x1s_ref, sp_ref, g_ref, gf_ref, win_ref, wgrp_ref, ps_ref,
                      wout_ref, ys_ref, nps_ref, acc_ref, hn_ref, stage):
    w = POOL_WINDOWS[g]
    c = _chunk(g)
    m = nt * nb
    if g == 0:
        x1 = _time_major(x1s_ref, nt)
        acc_ref[...] = x1
        hn_ref[...] = _rmsnorm(x1, g_ref[1:2, :]).astype(BF16)
    hn1 = hn_ref[...]
    raw_refs, slot = stage
    raw_ref = raw_refs[g % len(raw_refs)]
    raw_ref[slot, 0, 0:m, :] = _dot(hn1, win_ref[:, _chunk(g, 0)])
    raw_ref[slot, 1, 0:m, :] = _dot(hn1, win_ref[:, _chunk(g, 1)])
    u = raw_ref[slot, 0, 0:m, :]

    def full(r):
        if r < POOL_HIST:
            return sp_ref[r]
        return u[(r - POOL_HIST) * nb:(r - POOL_HIST + 1) * nb]

    parts = []
    for t in range(nt):
        win = full(POOL_HIST + t)
        for k in range(1, w):
            win = win + full(POOL_HIST + t - k)
        inv = 1.0 / float(min(w, start_pos + t + 1))
        parts.append(win * inv - full(POOL_HIST + t))
    p = jnp.concatenate(parts, axis=0).astype(BF16)
    raw_ref[slot, 2, 0:m, :] = _dot(p, wgrp_ref[_chunk(g), :])
    q, z = raw_ref[slot, 2, 0:m, :], raw_ref[slot, 1, 0:m, :]
    y = (q * ps_ref[:, c] * _silu(z)).astype(BF16)
    acc_ref[...] += _dot(y, wout_ref[c, :])

    for r in range(POOL_HIST):
        nps_ref[r] = full(r + nt)
    if g == N_POOL_GROUPS - 1:
        ys = _rmsnorm(acc_ref[...], gf_ref[...])
        for t in range(nt):
            ys_ref[:, t, :] = ys[t * nb:(t + 1) * nb]


def _pool_layer_kernel(x1p_ref, x1s_ref, sp_ref, g_ref, gf_ref, win_hbm, wgrp_hbm, ps_ref, wout_hbm, slot_ref,
                       hn1p_ref, yp_ref, ys_ref, npp_ref, nps_ref, vh_ref, ybuf_ref, acc_ref, hn_ref,
                       rawa_ref, rawb_ref, win_ref, wgrp_ref, wout_ref, wug_ref, tails_ref, wsem_ref,
                       *, tm, tiles_per_seq, n_prompt_steps, nb, nt, start_pos):
    s = pl.program_id(0)
    raw_refs = (rawa_ref, rawb_ref)
    stage = (raw_refs, slot_ref[0])
    phase = jnp.maximum(s - n_prompt_steps, 0) % N_POOL_GROUPS

    @pl.when(s == 0)
    def _():
        def fuse_group_weights():
            for g in range(N_POOL_GROUPS):
                c = _chunk(g)
                wug_ref[:, c] = _dot(win_ref[:, c], wgrp_ref[c, :]).astype(BF16)

        u_cols, z_cols = pl.ds(0, EXPAND_WIDTH), pl.ds(EXPAND_WIDTH, EXPAND_WIDTH)
        copies = [pltpu.make_async_copy(src, dst, wsem_ref.at[k]) for k, (src, dst) in
                  enumerate(((win_hbm.at[:, u_cols], win_ref.at[:, u_cols]), (wgrp_hbm, wgrp_ref),
                             (win_hbm.at[:, z_cols], win_ref.at[:, z_cols]), (wout_hbm, wout_ref)))]
        for copy in copies:
            copy.start()
        copies[0].wait()
        copies[1].wait()
        fuse_group_weights()
        copies[2].wait()
        copies[3].wait()

    @pl.when(s < n_prompt_steps)
    def _():
        i = s % tiles_per_seq
        _pool_prompt_tile(i, s // tiles_per_seq, tm, x1p_ref, hn1p_ref, gf_ref, win_ref, wug_ref, ps_ref, wout_ref,
                          yp_ref, tails_ref, vh_ref, ybuf_ref, stage, i == tiles_per_seq - 1)

    @pl.when(s == n_prompt_steps)
    def _():
        _pool_prompt_state(tails_ref, win_ref, npp_ref)

    for g in range(N_POOL_GROUPS):
        @pl.when(jnp.logical_and(s >= n_prompt_steps, phase == g))
        def _(g=g):
            _pool_sample_step(g, nb, nt, start_pos, x1s_ref, sp_ref, g_ref, gf_ref, win_ref, wgrp_ref,
                              ps_ref, wout_ref, ys_ref, nps_ref, acc_ref, hn_ref, stage)


def _resident(shape):
    zeros = (0,) * len(shape)
    return pl.BlockSpec(shape, lambda s: zeros, pipeline_mode=pl.Buffered(1))


def _layer_calls(xp, xs, sc, sp, g, gf, win0, cw, cb, wout0, win1, wgrp, ps, wout1):
    B, S, D = xp.shape
    NB, NT, _ = xs.shape
    E = EXPAND_WIDTH
    tm = PROMPT_TILE
    assert S % tm == 0 and tm >= POOL_CARRY_ROWS and NB % SAMPLE_SPLIT == 0
    tiles = S // tm
    n_prompt = B * tiles
    nb = NB // SAMPLE_SPLIT
    m = NT * nb
    assert nb % SUBLANES == 0 and m <= tm

    def prompt_tile(s):
        sc_ = jnp.minimum(s, n_prompt - 1)
        return (sc_ // tiles, sc_ % tiles, 0)

    def prompt_seq(s):
        return (jnp.minimum(s, n_prompt - 1) // tiles, 0, 0)

    def conv_part(s):
        return (jnp.maximum(s - n_prompt, 0), 0, 0)

    def pool_part(s):
        return (jnp.maximum(s - n_prompt, 0) // N_POOL_GROUPS, 0, 0)

    def pool_group(s):
        q = jnp.maximum(s - n_prompt, 0)
        return (0, q // N_POOL_GROUPS, q % N_POOL_GROUPS)

    tile_bytes = 4 * tm * D * 4 + 2 * tm * D * 2
    temp_bytes = 16 * tm * CHUNK * 4 + tm * E * 2
    stage_bytes = 2 * 4 * tm * CHUNK * 4
    headroom = 4 << 20
    stage_slot = jnp.zeros((1,), jnp.int32)
    hbm = pl.BlockSpec(memory_space=pl.ANY)
    conv_planes, pool_planes = 4, 3

    conv_sample_bytes = 2 * 4 * (2 * NT * nb * D + 2 * (CONV_WIDTH - 1) * nb * E)
    next_bytes = 2 * (4 + 2) * (win1.size + wgrp.size + wout1.size) // n_prompt
    conv_limit = min(2 * (win0.size + wout0.size) + tile_bytes + temp_bytes + stage_bytes + next_bytes
                     + conv_sample_bytes + headroom, VMEM_PHYSICAL_BYTES - headroom)
    next_weights = (win1, wgrp, wout1)
    assert all(w.shape[0] % (n_prompt * 2 * SUBLANES) == 0 for w in next_weights)

    def slab(w):
        return pl.BlockSpec((w.shape[0] // n_prompt, w.shape[1]), lambda s: (jnp.minimum(s, n_prompt - 1), 0))

    x1p, x1s, ncp, ncs, win1, wgrp, wout1, hn1p = pl.pallas_call(
        functools.partial(_conv_layer_kernel, tm=tm, tiles_per_seq=tiles, n_prompt_steps=n_prompt,
                          nb=nb, nt=NT),
        grid=(n_prompt + SAMPLE_SPLIT,),
        in_specs=[
            pl.BlockSpec((None, tm, D), prompt_tile),
            pl.BlockSpec((nb, NT, D), conv_part),
            pl.BlockSpec((nb, CONV_WIDTH - 1, E), conv_part),
            _resident(g.shape), hbm, _resident(cw.shape), _resident(cb.shape), hbm,
            pl.BlockSpec(memory_space=pltpu.SMEM),
        ] + [slab(w) for w in next_weights],
        out_specs=[
            pl.BlockSpec((None, tm, D), prompt_tile),
            pl.BlockSpec((nb, NT, D), conv_part),
            pl.BlockSpec((None, CONV_WIDTH - 1, E), prompt_seq),
            pl.BlockSpec((nb, CONV_WIDTH - 1, E), conv_part),
        ] + [slab(w) for w in next_weights] + [pl.BlockSpec((None, tm, D), prompt_tile)],
        out_shape=[
            jax.ShapeDtypeStruct((B, S, D), F32),
            jax.ShapeDtypeStruct((NB, NT, D), F32),
            jax.ShapeDtypeStruct((B, CONV_WIDTH - 1, E), F32),
            jax.ShapeDtypeStruct((NB, CONV_WIDTH - 1, E), F32),
        ] + [jax.ShapeDtypeStruct(w.shape, BF16) for w in next_weights] + [jax.ShapeDtypeStruct((B, S, D), BF16)],
        scratch_shapes=[
            pltpu.VMEM((CONV_CARRY_ROWS, E), F32),
            pltpu.VMEM((tm, E), BF16),
            pltpu.VMEM((1, conv_planes, tm, CHUNK), F32),
            pltpu.VMEM((1, conv_planes, tm, CHUNK), F32),
            pltpu.VMEM(win0.shape, BF16),
            pltpu.VMEM(wout0.shape, BF16),
            pltpu.SemaphoreType.DMA((2 * conv_planes,)),
        ],
        compiler_params=pltpu.CompilerParams(
            dimension_semantics=("arbitrary",), vmem_limit_bytes=conv_limit),
        name="conv_layer",
    )(xp, xs, sc, g, win0, cw, cb, wout0, stage_slot, *next_weights)

    pool_sample_bytes = 2 * 4 * (2 * NT * nb * D + 2 * nb * (POOL_HIST + 1) * CHUNK) + m * D * 6
    pool_limit = min(2 * (win1.size + wgrp.size + wout1.size + D * E) + tile_bytes + temp_bytes
                     + stage_bytes + pool_sample_bytes + headroom, VMEM_PHYSICAL_BYTES - headroom)
    yp, ys, npp, nps = pl.pallas_call(
        functools.partial(_pool_layer_kernel, tm=tm, tiles_per_seq=tiles, n_prompt_steps=n_prompt,
                          nb=nb, nt=NT, start_pos=PAST_LEN),
        grid=(n_prompt + SAMPLE_SPLIT * N_POOL_GROUPS,),
        in_specs=[
            pl.BlockSpec((None, tm, D), prompt_tile),
            pl.BlockSpec((nb, NT, D), pool_part),
            pl.BlockSpec((POOL_HIST, nb, CHUNK), pool_group),
            _resident(g.shape), _resident(gf.shape), hbm, hbm, _resident(ps.shape), hbm,
            pl.BlockSpec(memory_space=pltpu.SMEM),
            pl.BlockSpec((None, tm, D), prompt_tile),
        ],
        out_specs=[
            pl.BlockSpec((None, tm, D), prompt_tile),
            pl.BlockSpec((nb, NT, D), pool_part),
            pl.BlockSpec((POOL_HIST, B, E), lambda s: (0, 0, 0)),
            pl.BlockSpec((POOL_HIST, nb, CHUNK), pool_group),
        ],
        out_shape=[
            jax.ShapeDtypeStruct((B, S, D), F32),
            jax.ShapeDtypeStruct((NB, NT, D), F32),
            jax.ShapeDtypeStruct((POOL_HIST, B, E), F32),
            jax.ShapeDtypeStruct((POOL_HIST, NB, E), F32),
        ],
        scratch_shapes=[
            pltpu.VMEM((POOL_CARRY_ROWS, E), F32),
            pltpu.VMEM((tm, E), BF16),
            pltpu.VMEM((m, D), F32),
            pltpu.VMEM((m, D), BF16),
            pltpu.VMEM((1, pool_planes, tm, CHUNK), F32),
            pltpu.VMEM((1, pool_planes, tm, CHUNK), F32),
            pltpu.VMEM(win1.shape, BF16),
            pltpu.VMEM(wgrp.shape, BF16),
            pltpu.VMEM(wout1.shape, BF16),
            pltpu.VMEM((D, E), BF16),
            pltpu.VMEM((B * POOL_CARRY_ROWS, D), BF16),
            pltpu.SemaphoreType.DMA((2 * pool_planes,)),
        ],
        compiler_params=pltpu.CompilerParams(
            dimension_semantics=("arbitrary",), vmem_limit_bytes=pool_limit),
        name="pool_layer",
    )(x1p, x1s, sp, g, gf, win1, wgrp, ps, wout1, stage_slot, hn1p)
    return yp, ys, ncp, ncs, npp, nps


def kernel(x_prompt, x_sample, state_conv, state_pool, norm_g, final_norm_g,
           conv_w_in, conv_w, conv_b, conv_w_out,
           pool_w_in, pool_w_grp, pool_scale, pool_w_out):
    assert norm_g.shape[0] == 2 and conv_w_in.shape[0] == 1 and pool_w_in.shape[0] == 1
    gf = final_norm_g.reshape(1, D_MODEL)
    win0, wout0, win1, wout1 = conv_w_in[0], conv_w_out[0], pool_w_in[0], pool_w_out[0]
    wgrp = pool_w_grp[0].reshape(EXPAND_WIDTH, POOL_GROUP_WIDTH)
    yp, ys, ncp, ncs, npp, nps = _layer_calls(
        x_prompt, x_sample, state_conv[0], jnp.transpose(state_pool[0], (1, 0, 2)), norm_g, gf,
        win0, conv_w.reshape(1, CONV_WIDTH * EXPAND_WIDTH), conv_b, wout0, win1, wgrp, pool_scale, wout1)
    return (yp, ys, ncp[None], ncs[None],
            jnp.transpose(npp, (1, 0, 2))[None], jnp.transpose(nps, (1, 0, 2))[None])
```

```python
import functools

import jax
import jax.numpy as jnp
from jax import lax
from jax.experimental import pallas as pl
from jax.experimental.pallas import tpu as pltpu

D_MODEL = 1024
EXPAND_WIDTH = 2048
CONV_WIDTH = 3
POOL_WINDOWS = (2, 4, 8, 16)
N_POOL_GROUPS = len(POOL_WINDOWS)
POOL_GROUP_WIDTH = EXPAND_WIDTH // N_POOL_GROUPS
POOL_HIST = max(POOL_WINDOWS) - 1
PAST_LEN = 16384
RMS_EPS = 1e-6
LOG2_E = 1.4426950408889634

LANES = 128
SUBLANES = 8
CONV_CARRY_ROWS = SUBLANES
POOL_CARRY_ROWS = 2 * SUBLANES
CHUNK = POOL_GROUP_WIDTH
N_CHUNKS = EXPAND_WIDTH // CHUNK
PROMPT_TILE = 512
SAMPLE_SPLIT = 2
VMEM_PHYSICAL_BYTES = 64 * 1024 * 1024

BF16 = jnp.bfloat16
F32 = jnp.float32


def _dot(a, b):
    return jnp.dot(a, b, preferred_element_type=F32)


def _rmsnorm(x, g):
    r = lax.rsqrt(jnp.mean(x * x, axis=-1, keepdims=True) + RMS_EPS)
    return x * r * g


def _silu(z):
    return z / (1.0 + jnp.exp2(z * (-LOG2_E)))


def _chunk(j, section=0):
    lo = section * EXPAND_WIDTH + j * CHUNK
    return slice(lo, lo + CHUNK)


def _weight_pieces(src_hbm, dst_ref, piece):
    pr, pc = piece
    rows, cols = src_hbm.shape
    assert rows % pr == 0 and cols % pc == 0
    pairs = []
    for i in range(rows // pr):
        for j in range(cols // pc):
            r, c = pl.ds(i * pr, pr), pl.ds(j * pc, pc)
            pairs.append((src_hbm.at[r, c], dst_ref.at[r, c]))
    return pairs


def _load_weights_as_bf16(weights, raw_refs, sem_ref):
    slots = [r.at[0, k] for r in raw_refs for k in range(r.shape[1])]
    pieces = []
    for src_hbm, dst_ref in weights:
        pieces += _weight_pieces(src_hbm, dst_ref, slots[0].shape)
    n, r = len(pieces), len(slots)

    def copy(k):
        return pltpu.make_async_copy(pieces[k][0], slots[k % r], sem_ref.at[k % r])

    for k in range(min(r, n)):
        copy(k).start(priority=k % 2)
    for k in range(n):
        copy(k).wait()
        pieces[k][1][...] = slots[k % r][...].astype(BF16)
        if k + r < n:
            copy(k + r).start(priority=(k + r) % 2)


def _time_major(ref, nt):
    return jnp.concatenate([ref[:, t, :] for t in range(nt)], axis=0)


def _row_parts(m, parts):
    return [slice(h * (m // parts), (h + 1) * (m // parts)) for h in range(parts)]


def _conv_gates(hn, win_ref, j, stage, parts=1):
    raw_refs, slot = stage
    raw_ref = raw_refs[j % len(raw_refs)]
    m = hn.shape[0]
    for r in _row_parts(m, parts):
        for sec in range(4):
            raw_ref[slot, sec, r, :] = _dot(hn[r], win_ref[:, _chunk(j, sec)])
    gb, gc, v, z = (raw_ref[slot, sec, 0:m, :] for sec in range(4))
    return gb, gc * v, z


def _conv_prompt_tile(i, tm, x_ref, g_ref, win_ref, cw_ref, cb_ref, wout_ref,
                      x1_ref, nconv_ref, cvh_ref, ybuf_ref, stage, last_tile):
    @pl.when(i == 0)
    def _():
        cvh_ref[...] = jnp.zeros_like(cvh_ref)

    x = x_ref[...]
    hn = _rmsnorm(x, g_ref[0:1, :]).astype(BF16)
    for j in range(N_CHUNKS):
        c = _chunk(j)
        gb, cv, z = _conv_gates(hn, win_ref, j, stage, parts=2 if j == 0 else 1)
        ext = jnp.concatenate([cvh_ref[:, c], cv], axis=0)
        cm1 = pltpu.roll(ext, 1, 0)[CONV_CARRY_ROWS:]
        cm2 = pltpu.roll(ext, 2, 0)[CONV_CARRY_ROWS:]
        conv = cb_ref[:, c] + cm2 * cw_ref[:, _chunk(j, 0)]
        conv = conv + cm1 * cw_ref[:, _chunk(j, 1)]
        conv = conv + cv * cw_ref[:, _chunk(j, 2)]
        ybuf_ref[0:tm, c] = (gb * conv * _silu(z)).astype(BF16)
        cvh_ref[:, c] = cv[tm - CONV_CARRY_ROWS:]
    for r in _row_parts(tm, 2):
        x1_ref[r, :] = x[r] + _dot(ybuf_ref[r, :], wout_ref[...])

    @pl.when(last_tile)
    def _():
        nconv_ref[...] = cvh_ref[CONV_CARRY_ROWS - (CONV_WIDTH - 1):, :]


def _conv_sample_step(nb, nt, xs_ref, sc_ref, g_ref, win_ref, cw_ref, cb_ref, wout_ref,
                      x1s_ref, ncs_ref, ybuf_ref, stage):
    m = nt * nb
    xs = _time_major(xs_ref, nt)
    hn = _rmsnorm(xs, g_ref[0:1, :]).astype(BF16)
    for j in range(N_CHUNKS):
        c = _chunk(j)
        gb, cv, z = _conv_gates(hn, win_ref, j, stage)
        full = jnp.concatenate([sc_ref[:, k, c] for k in range(CONV_WIDTH - 1)] + [cv], axis=0)
        conv = cb_ref[:, c]
        for k in range(CONV_WIDTH):
            conv = conv + full[k * nb:(k + nt) * nb] * cw_ref[:, _chunk(j, k)]
        ybuf_ref[0:m, c] = (gb * conv * _silu(z)).astype(BF16)
        for k in range(CONV_WIDTH - 1):
            ncs_ref[:, k, c] = full[(nt + k) * nb:(nt + k + 1) * nb]
    x1 = xs + _dot(ybuf_ref[0:m, :], wout_ref[...])
    for t in range(nt):
        x1s_ref[:, t, :] = x1[t * nb:(t + 1) * nb]


def _conv_layer_kernel(xp_ref, xs_ref, sc_ref, g_ref, win_hbm, cw_ref, cb_ref, wout_hbm, slot_ref,
                       nxt_in_ref, nxt_grp_ref, nxt_out_ref,
                       x1p_ref, x1s_ref, ncp_ref, ncs_ref, nxt_in_bf_ref, nxt_grp_bf_ref, nxt_out_bf_ref,
                       cvh_ref, ybuf_ref, rawa_ref, rawb_ref, win_ref, wout_ref, wsem_ref,
                       *, tm, tiles_per_seq, n_prompt_steps, nb, nt):
    s = pl.program_id(0)
    raw_refs = (rawa_ref, rawb_ref)
    stage = (raw_refs, slot_ref[0])

    @pl.when(s == 0)
    def _():
        _load_weights_as_bf16([(win_hbm, win_ref), (wout_hbm, wout_ref)], raw_refs, wsem_ref)

    @pl.when(s < n_prompt_steps)
    def _():
        i = s % tiles_per_seq
        _conv_prompt_tile(i, tm, xp_ref, g_ref, win_ref, cw_ref, cb_ref, wout_ref,
                          x1p_ref, ncp_ref, cvh_ref, ybuf_ref, stage, i == tiles_per_seq - 1)
        for src_ref, dst_ref in ((nxt_in_ref, nxt_in_bf_ref), (nxt_grp_ref, nxt_grp_bf_ref),
                                 (nxt_out_ref, nxt_out_bf_ref)):
            dst_ref[...] = src_ref[...].astype(BF16)

    @pl.when(s >= n_prompt_steps)
    def _():
        _conv_sample_step(nb, nt, xs_ref, sc_ref, g_ref, win_ref, cw_ref, cb_ref, wout_ref,
                          x1s_ref, ncs_ref, ybuf_ref, stage)


def _pool_prompt_tile(i, seq, tm, x1_ref, g_ref, gf_ref, win_ref, wug_ref, ps_ref, wout_ref,
                      y_ref, tails_ref, vh_ref, ybuf_ref, stage, last_tile):
    @pl.when(i == 0)
    def _():
        vh_ref[...] = jnp.zeros_like(vh_ref)

    x1 = x1_ref[...]
    hn1 = _rmsnorm(x1, g_ref[1:2, :]).astype(BF16)
    seen = (i * tm + 1 + lax.broadcasted_iota(jnp.int32, (tm, LANES), 0)).astype(F32)
    raw_refs, slot = stage
    for g, w in enumerate(POOL_WINDOWS):
        c = _chunk(g)
        raw_ref = raw_refs[g % len(raw_refs)]
        for r in _row_parts(tm, 2 if g == 0 else 1):
            raw_ref[slot, 0, r, :] = _dot(hn1[r], wug_ref[:, c])
            raw_ref[slot, 1, r, :] = _dot(hn1[r], win_ref[:, _chunk(g, 1)])
        v = raw_ref[slot, 0]
        s = jnp.concatenate([vh_ref[:, c], v], axis=0)
        shift = 1
        while shift < w:
            s = s + pltpu.roll(s, shift, 0)
            shift *= 2
        inv = 1.0 / jnp.minimum(jnp.float32(w), seen)
        inv = jnp.concatenate([inv] * (CHUNK // LANES), axis=1)
        q = s[POOL_CARRY_ROWS:] * inv - v
        z = raw_ref[slot, 1]
        ybuf_ref[0:tm, c] = (q * ps_ref[:, c] * _silu(z)).astype(BF16)
        vh_ref[:, c] = v[tm - POOL_CARRY_ROWS:]
    kh = EXPAND_WIDTH // 2
    rows = _row_parts(tm, 2)
    lo = [_dot(ybuf_ref[r, 0:kh], wout_ref[0:kh, :]) for r in rows]
    hi = [_dot(ybuf_ref[r, kh:], wout_ref[kh:, :]) for r in rows]
    for r, a, b in zip(rows, lo, hi):
        y_ref[r, :] = _rmsnorm(x1_ref[r, :] + (a + b), gf_ref[...])

    @pl.when(last_tile)
    def _():
        tail = slice(tm - POOL_CARRY_ROWS, tm)
        at = pl.multiple_of(seq * POOL_CARRY_ROWS, POOL_CARRY_ROWS)
        tails_ref[pl.ds(at, POOL_CARRY_ROWS), :] = _rmsnorm(x1_ref[tail, :], g_ref[1:2, :]).astype(BF16)


def _pool_prompt_state(tails_ref, win_ref, npool_ref):
    u_tails = _dot(tails_ref[...], win_ref[:, 0:EXPAND_WIDTH])
    for b in range(npool_ref.shape[1]):
        npool_ref[:, b, :] = u_tails[(b + 1) * POOL_CARRY_ROWS - POOL_HIST:(b + 1) * POOL_CARRY_ROWS, :]


def _pool_sample_step(g, nb, nt, start_pos, x1s_ref, sp_ref, g_ref, gf_ref, win_ref, wgrp_ref, ps_ref,
                      wout_ref, ys_ref, nps_ref, acc_ref, hn_ref, stage):
    w = POOL_WINDOWS[g]
    c = _chunk(g)
    m = nt * nb
    if g == 0:
        x1 = _time_major(x1s_ref, nt)
        acc_ref[...] = x1
        hn_ref[...] = _rmsnorm(x1, g_ref[1:2, :]).astype(BF16)
    hn1 = hn_ref[...]
    raw_refs, slot = stage
    raw_ref = raw_refs[g % len(raw_refs)]
    raw_ref[slot, 0, 0:m, :] = _dot(hn1, win_ref[:, _chunk(g, 0)])
    raw_ref[slot, 1, 0:m, :] = _dot(hn1, win_ref[:, _chunk(g, 1)])
    u = raw_ref[slot, 0, 0:m, :]

    def full(r):
        if r < POOL_HIST:
            return sp_ref[r]
        return u[(r - POOL_HIST) * nb:(r - POOL_HIST + 1) * nb]

    parts = []
    for t in range(nt):
        win = full(POOL_HIST + t)
        for k in range(1, w):
            win = win + full(POOL_HIST + t - k)
        inv = 1.0 / float(min(w, start_pos + t + 1))
        parts.append(win * inv - full(POOL_HIST + t))
    p = jnp.concatenate(parts, axis=0).astype(BF16)
    raw_ref[slot, 2, 0:m, :] = _dot(p, wgrp_ref[_chunk(g), :])
    q, z = raw_ref[slot, 2, 0:m, :], raw_ref[slot, 1, 0:m, :]
    y = (q * ps_ref[:, c] * _silu(z)).astype(BF16)
    acc_ref[...] += _dot(y, wout_ref[c, :])

    for r in range(POOL_HIST):
        nps_ref[r] = full(r + nt)
    if g == N_POOL_GROUPS - 1:
        ys = _rmsnorm(acc_ref[...], gf_ref[...])
        for t in range(nt):
            ys_ref[:, t, :] = ys[t * nb:(t + 1) * nb]


def _pool_layer_kernel(x1p_ref, x1s_ref, sp_ref, g_ref, gf_ref, win_hbm, wgrp_hbm, ps_ref, wout_hbm, slot_ref,
                       yp_ref, ys_ref, npp_ref, nps_ref, vh_ref, ybuf_ref, acc_ref, hn_ref,
                       rawa_ref, rawb_ref, win_ref, wgrp_ref, wout_ref, wug_ref, tails_ref, wsem_ref,
                       *, tm, tiles_per_seq, n_prompt_steps, nb, nt, start_pos):
    s = pl.program_id(0)
    raw_refs = (rawa_ref, rawb_ref)
    stage = (raw_refs, slot_ref[0])
    phase = jnp.maximum(s - n_prompt_steps, 0) % N_POOL_GROUPS

    @pl.when(s == 0)
    def _():
        def fuse_group_weights():
            for g in range(N_POOL_GROUPS):
                c = _chunk(g)
                wug_ref[:, c] = _dot(win_ref[:, c], wgrp_ref[c, :]).astype(BF16)

        u_cols, z_cols = pl.ds(0, EXPAND_WIDTH), pl.ds(EXPAND_WIDTH, EXPAND_WIDTH)
        copies = [pltpu.make_async_copy(src, dst, wsem_ref.at[k]) for k, (src, dst) in
                  enumerate(((win_hbm.at[:, u_cols], win_ref.at[:, u_cols]), (wgrp_hbm, wgrp_ref),
                             (win_hbm.at[:, z_cols], win_ref.at[:, z_cols]), (wout_hbm, wout_ref)))]
        for k, copy in enumerate(copies):
            copy.start(priority=k % 2)
        copies[0].wait()
        copies[1].wait()
        fuse_group_weights()
        copies[2].wait()
        copies[3].wait()

    @pl.when(s < n_prompt_steps)
    def _():
        i = s % tiles_per_seq
        _pool_prompt_tile(i, s // tiles_per_seq, tm, x1p_ref, g_ref, gf_ref, win_ref, wug_ref, ps_ref, wout_ref,
                          yp_ref, tails_ref, vh_ref, ybuf_ref, stage, i == tiles_per_seq - 1)

    @pl.when(s == n_prompt_steps)
    def _():
        _pool_prompt_state(tails_ref, win_ref, npp_ref)

    for g in range(N_POOL_GROUPS):
        @pl.when(jnp.logical_and(s >= n_prompt_steps, phase == g))
        def _(g=g):
            _pool_sample_step(g, nb, nt, start_pos, x1s_ref, sp_ref, g_ref, gf_ref, win_ref, wgrp_ref,
                              ps_ref, wout_ref, ys_ref, nps_ref, acc_ref, hn_ref, stage)


def _resident(shape):
    zeros = (0,) * len(shape)
    return pl.BlockSpec(shape, lambda s: zeros, pipeline_mode=pl.Buffered(1))


def _layer_calls(xp, xs, sc, sp, g, gf, win0, cw, cb, wout0, win1, wgrp, ps, wout1):
    B, S, D = xp.shape
    NB, NT, _ = xs.shape
    E = EXPAND_WIDTH
    tm = PROMPT_TILE
    assert S % tm == 0 and tm >= POOL_CARRY_ROWS and NB % SAMPLE_SPLIT == 0
    tiles = S // tm
    n_prompt = B * tiles
    nb = NB // SAMPLE_SPLIT
    m = NT * nb
    assert nb % SUBLANES == 0 and m <= tm

    def prompt_tile(s):
        sc_ = jnp.minimum(s, n_prompt - 1)
        return (sc_ // tiles, sc_ % tiles, 0)

    def prompt_seq(s):
        return (jnp.minimum(s, n_prompt - 1) // tiles, 0, 0)

    def conv_part(s):
        return (jnp.maximum(s - n_prompt, 0), 0, 0)

    def pool_part(s):
        return (jnp.maximum(s - n_prompt, 0) // N_POOL_GROUPS, 0, 0)

    def pool_group(s):
        q = jnp.maximum(s - n_prompt, 0)
        return (0, q // N_POOL_GROUPS, q % N_POOL_GROUPS)

    tile_bytes = 4 * tm * D * 4
    temp_bytes = 16 * tm * CHUNK * 4 + tm * E * 2
    stage_bytes = 2 * 4 * tm * CHUNK * 4
    headroom = 4 << 20
    stage_slot = jnp.zeros((1,), jnp.int32)
    hbm = pl.BlockSpec(memory_space=pl.ANY)
    conv_planes, pool_planes = 4, 3

    conv_sample_bytes = 2 * 4 * (2 * NT * nb * D + 2 * (CONV_WIDTH - 1) * nb * E)
    next_bytes = 2 * (4 + 2) * (win1.size + wgrp.size + wout1.size) // n_prompt
    conv_limit = min(2 * (win0.size + wout0.size) + tile_bytes + temp_bytes + stage_bytes + next_bytes
                     + conv_sample_bytes + headroom, VMEM_PHYSICAL_BYTES - headroom)
    next_weights = (win1, wgrp, wout1)
    assert all(w.shape[0] % (n_prompt * 2 * SUBLANES) == 0 for w in next_weights)

    def slab(w):
        return pl.BlockSpec((w.shape[0] // n_prompt, w.shape[1]), lambda s: (jnp.minimum(s, n_prompt - 1), 0))

    x1p, x1s, ncp, ncs, win1, wgrp, wout1 = pl.pallas_call(
        functools.partial(_conv_layer_kernel, tm=tm, tiles_per_seq=tiles, n_prompt_steps=n_prompt,
                          nb=nb, nt=NT),
        grid=(n_prompt + SAMPLE_SPLIT,),
        in_specs=[
            pl.BlockSpec((None, tm, D), prompt_tile),
            pl.BlockSpec((nb, NT, D), conv_part),
            pl.BlockSpec((nb, CONV_WIDTH - 1, E), conv_part),
            _resident(g.shape), hbm, _resident(cw.shape), _resident(cb.shape), hbm,
            pl.BlockSpec(memory_space=pltpu.SMEM),
        ] + [slab(w) for w in next_weights],
        out_specs=[
            pl.BlockSpec((None, tm, D), prompt_tile),
            pl.BlockSpec((nb, NT, D), conv_part),
            pl.BlockSpec((None, CONV_WIDTH - 1, E), prompt_seq),
            pl.BlockSpec((nb, CONV_WIDTH - 1, E), conv_part),
        ] + [slab(w) for w in next_weights],
        out_shape=[
            jax.ShapeDtypeStruct((B, S, D), F32),
            jax.ShapeDtypeStruct((NB, NT, D), F32),
            jax.ShapeDtypeStruct((B, CONV_WIDTH - 1, E), F32),
            jax.ShapeDtypeStruct((NB, CONV_WIDTH - 1, E), F32),
        ] + [jax.ShapeDtypeStruct(w.shape, BF16) for w in next_weights],
        scratch_shapes=[
            pltpu.VMEM((CONV_CARRY_ROWS, E), F32),
            pltpu.VMEM((tm, E), BF16),
            pltpu.VMEM((1, conv_planes, tm, CHUNK), F32),
            pltpu.VMEM((1, conv_planes, tm, CHUNK), F32),
            pltpu.VMEM(win0.shape, BF16),
            pltpu.VMEM(wout0.shape, BF16),
            pltpu.SemaphoreType.DMA((2 * conv_planes,)),
        ],
        compiler_params=pltpu.CompilerParams(
            dimension_semantics=("arbitrary",), vmem_limit_bytes=conv_limit),
        name="conv_layer",
    )(xp, xs, sc, g, win0, cw, cb, wout0, stage_slot, *next_weights)

    pool_sample_bytes = 2 * 4 * (2 * NT * nb * D + 2 * nb * (POOL_HIST + 1) * CHUNK) + m * D * 6
    pool_limit = min(2 * (win1.size + wgrp.size + wout1.size + D * E) + tile_bytes + temp_bytes
                     + stage_bytes + pool_sample_bytes + headroom, VMEM_PHYSICAL_BYTES - headroom)
    yp, ys, npp, nps = pl.pallas_call(
        functools.partial(_pool_layer_kernel, tm=tm, tiles_per_seq=tiles, n_prompt_steps=n_prompt,
                          nb=nb, nt=NT, start_pos=PAST_LEN),
        grid=(n_prompt + SAMPLE_SPLIT * N_POOL_GROUPS,),
        in_specs=[
            pl.BlockSpec((None, tm, D), prompt_tile),
            pl.BlockSpec((nb, NT, D), pool_part),
            pl.BlockSpec((POOL_HIST, nb, CHUNK), pool_group),
            _resident(g.shape), _resident(gf.shape), hbm, hbm, _resident(ps.shape), hbm,
            pl.BlockSpec(memory_space=pltpu.SMEM),
        ],
        out_specs=[
            pl.BlockSpec((None, tm, D), prompt_tile),
            pl.BlockSpec((nb, NT, D), pool_part),
            pl.BlockSpec((POOL_HIST, B, E), lambda s: (0, 0, 0)),
            pl.BlockSpec((POOL_HIST, nb, CHUNK), pool_group),
        ],
        out_shape=[
            jax.ShapeDtypeStruct((B, S, D), F32),
            jax.ShapeDtypeStruct((NB, NT, D), F32),
            jax.ShapeDtypeStruct((POOL_HIST, B, E), F32),
            jax.ShapeDtypeStruct((POOL_HIST, NB, E), F32),
        ],
        scratch_shapes=[
            pltpu.VMEM((POOL_CARRY_ROWS, E), F32),
            pltpu.VMEM((tm, E), BF16),
            pltpu.VMEM((m, D), F32),
            pltpu.VMEM((m, D), BF16),
            pltpu.VMEM((1, pool_planes, tm, CHUNK), F32),
            pltpu.VMEM((1, pool_planes, tm, CHUNK), F32),
            pltpu.VMEM(win1.shape, BF16),
            pltpu.VMEM(wgrp.shape, BF16),
            pltpu.VMEM(wout1.shape, BF16),
            pltpu.VMEM((D, E), BF16),
            pltpu.VMEM((B * POOL_CARRY_ROWS, D), BF16),
            pltpu.SemaphoreType.DMA((2 * pool_planes,)),
        ],
        compiler_params=pltpu.CompilerParams(
            dimension_semantics=("arbitrary",), vmem_limit_bytes=pool_limit),
        name="pool_layer",
    )(x1p, x1s, sp, g, gf, win1, wgrp, ps, wout1, stage_slot)
    return yp, ys, ncp, ncs, npp, nps


def kernel(x_prompt, x_sample, state_conv, state_pool, norm_g, final_norm_g,
           conv_w_in, conv_w, conv_b, conv_w_out,
           pool_w_in, pool_w_grp, pool_scale, pool_w_out):
    assert norm_g.shape[0] == 2 and conv_w_in.shape[0] == 1 and pool_w_in.shape[0] == 1
    gf = final_norm_g.reshape(1, D_MODEL)
    win0, wout0, win1, wout1 = conv_w_in[0], conv_w_out[0], pool_w_in[0], pool_w_out[0]
    wgrp = pool_w_grp[0].reshape(EXPAND_WIDTH, POOL_GROUP_WIDTH)
    yp, ys, ncp, ncs, npp, nps = _layer_calls(
        x_prompt, x_sample, state_conv[0], jnp.transpose(state_pool[0], (1, 0, 2)), norm_g, gf,
        win0, conv_w.reshape(1, CONV_WIDTH * EXPAND_WIDTH), conv_b, wout0, win1, wgrp, pool_scale, wout1)
    return (yp, ys, ncp[None], ncs[None],
            jnp.transpose(npp, (1, 0, 2))[None], jnp.transpose(nps, (1, 0, 2))[None])
```
